```python
import math
import jax, jax.numpy as jnp
from jax import lax
import numpy as np

D_MODEL = 1024
BATCH = 2
SEQ = 8192
DEPTH = 2
DEC_BATCH = 128
DEC_SEQ = 4
PAST_LEN = 2048
PAGE_SIZE = 128

N_EVEN = (DEPTH + 1) // 2
N_ODD = DEPTH // 2
D_A = D_MODEL // 2
N_BLK_A = 8
CONV_W = 4
RG_C = 8.0
DH_B = 64
H_B = (D_MODEL // 2) // DH_B
W_B = H_B * DH_B
DIL_PATTERNS = ((128, 1), (512, 4), (2048, 16))
WIN_MAX = 2048
DH_C = 64
H_C = (D_MODEL // 2) // (2 * DH_C)
W_C = H_C * 2 * DH_C
DK_D = 128
DV_D = 128
H_D = (D_MODEL // 2) // DV_D
W_DK = H_D * DK_D
W_D = H_D * DV_D
CHUNK_D = 16
N_BUCKETS = 32
T5_MAX_DIST = 2048
N_BIAS = H_B
Q_BLK = 128
IN_AB = 2 * D_A + 4 * W_B
OUT_AB = D_A + W_B
IN_CD = 4 * W_C + 2 * W_DK + 2 * W_D
OUT_CD = W_C + W_D
EPS = 1e-6
NEG = -1e30

kernel_name = 'hawk_longnet_diff_hgrn2_hybrid_step'


def _split(x, sizes):
    idx, acc = [], 0
    for s in sizes[:-1]:
        acc += s
        idx.append(acc)
    return jnp.split(x, idx, axis=-1)


def _rmsnorm(x, g):
    xf = x.astype(jnp.float32)
    y = xf * lax.rsqrt(jnp.mean(xf * xf, axis=-1, keepdims=True) + EPS)
    return (y * g.astype(jnp.float32)).astype(x.dtype)


def _t5_bucket(dist):
    n = jnp.maximum(dist, 0)
    exact = N_BUCKETS // 2
    nf = jnp.maximum(n, 1).astype(jnp.float32)
    large = exact + (jnp.log(nf / exact) / math.log(T5_MAX_DIST / exact) * (N_BUCKETS - exact)).astype(jnp.int32)
    large = jnp.minimum(large, N_BUCKETS - 1)
    return jnp.where(n < exact, n, large)


def _causal_conv(x, buf, w, b):
    t = x.shape[1]
    xp = jnp.concatenate([buf.astype(x.dtype), x], axis=1)
    y = xp[:, 0:t] * w[0]
    for j in range(1, CONV_W):
        y = y + xp[:, j:j + t] * w[j]
    return y + b, xp[:, t:]


def _lin_combine(e1, e2):
    a1, b1 = e1
    a2, b2 = e2
    return a1 * a2, a2 * b1 + b2


def _rglru(xc, h0, w_r, b_r, w_i, b_i, lam):
    b, t, c = xc.shape
    xb = xc.reshape(b, t, N_BLK_A, c // N_BLK_A)
    r = jax.nn.sigmoid((jnp.einsum('btgi,gij->btgj', xb, w_r) + b_r).astype(jnp.float32)).reshape(b, t, c)
    gi = jax.nn.sigmoid((jnp.einsum('btgi,gij->btgj', xb, w_i) + b_i).astype(jnp.float32)).reshape(b, t, c)
    log_a = -RG_C * r * jax.nn.softplus(-lam.astype(jnp.float32))
    a = jnp.exp(log_a)
    u = jnp.sqrt(-jnp.expm1(2.0 * log_a)) * (gi * xc.astype(jnp.float32))
    a_cum, h_in = lax.associative_scan(_lin_combine, (a, u), axis=1)
    h = a_cum * h0.astype(jnp.float32)[:, None, :] + h_in
    return h, h[:, -1]


def _dilated_band_prompt(q, k, v, bias_j, dil, n_back):
    b, s, h, dh = q.shape
    blk = n_back
    unit = dil * blk
    sp = -(-s // unit) * unit
    nb = sp // unit
    pad = ((0, 0), (0, sp - s), (0, 0), (0, 0))

    def split(z):
        return jnp.pad(z, pad).reshape(b, nb, blk, dil, h, dh)

    def with_prev(z):
        prev = jnp.pad(z, ((0, 0), (1, 0), (0, 0), (0, 0), (0, 0), (0, 0)))[:, :-1]
        return jnp.concatenate([prev, z], axis=2)

    qb = split(q)
    kc = with_prev(split(k))
    vc = with_prev(split(v))
    sc = jnp.einsum('bnirhd,bnjrhd->bnrhij', qb, kc, preferred_element_type=jnp.float32)
    rel = (jnp.arange(blk)[:, None] + blk) - jnp.arange(2 * blk)[None, :]
    band = (rel >= 0) & (rel <= n_back)
    not_pad = (jnp.arange(nb)[:, None, None] > 0) | (jnp.arange(2 * blk)[None, None, :] >= blk)
    valid = band[None] & not_pad
    bias = bias_j[:, jnp.clip(rel, 0, n_back)]
    sc = jnp.where(valid[None, :, None, None], sc + bias[None, None, None], NEG)
    m = jnp.max(sc, axis=-1, keepdims=True)
    p = jnp.exp(sc - m)
    l = jnp.sum(p, axis=-1)
    o = jnp.einsum('bnrhij,bnjrhd->bnirhd', p, vc)
    l_t = jnp.transpose(l, (0, 1, 4, 2, 3))
    o = o / l_t[..., None]
    lse = jnp.transpose(m[..., 0] + jnp.log(l), (0, 1, 4, 2, 3))
    return o.reshape(b, sp, h, dh)[:, :s], lse.reshape(b, sp, h)[:, :s]


def _dilated_gather_decode(q, k_ext, v_ext, bias_j, dil, n_back, l_buf):
    t = q.shape[1]
    idx = l_buf + jnp.arange(t)[:, None] - dil * jnp.arange(n_back + 1)[None, :]
    valid = idx >= 0
    idx = jnp.maximum(idx, 0)
    kg = k_ext[:, idx]
    vg = v_ext[:, idx]
    sc = jnp.einsum('bthd,btjhd->bhtj', q, kg, preferred_element_type=jnp.float32) + bias_j[None, :, None, :]
    sc = jnp.where(valid[None, None], sc, NEG)
    m = jnp.max(sc, axis=-1, keepdims=True)
    p = jnp.exp(sc - m)
    l = jnp.sum(p, axis=-1)
    o = jnp.einsum('bhtj,btjhd->bthd', p, vg) / jnp.transpose(l, (0, 2, 1))[..., None]
    lse = jnp.transpose(m[..., 0] + jnp.log(l), (0, 2, 1))
    return o, lse


def _mixer_ab(hn, w_in, conv_w, conv_b, w_r, b_r, w_i, b_i, lam, w_out, rel_bias, conv_buf, h0, win_kv):
    b, t, _ = hn.shape
    xa, ga, q, k, v, gb = _split(hn @ w_in, (D_A, D_A, W_B, W_B, W_B, W_B))
    xc, conv_new = _causal_conv(xa, conv_buf, conv_w, conv_b)
    h, h_last = _rglru(xc, h0, w_r, b_r, w_i, b_i, lam)
    ya = h.astype(hn.dtype) * jax.nn.silu(ga)
    q = q.reshape(b, t, H_B, DH_B) * (DH_B ** -0.5)
    k = k.reshape(b, t, H_B, DH_B)
    v = v.reshape(b, t, H_B, DH_B)
    if win_kv is not None:
        l_buf = win_kv[0].shape[1]
        k_ext = jnp.concatenate([win_kv[0].astype(k.dtype), k], axis=1)
        v_ext = jnp.concatenate([win_kv[1].astype(v.dtype), v], axis=1)
    outs, lses = [], []
    for win, dil in DIL_PATTERNS:
        n_back = win // dil
        bias_j = rel_bias[_t5_bucket(dil * jnp.arange(n_back + 1))].T
        if win_kv is None:
            o, lse = _dilated_band_prompt(q, k, v, bias_j, dil, n_back)
        else:
            o, lse = _dilated_gather_decode(q, k_ext, v_ext, bias_j, dil, n_back, l_buf)
        outs.append(o)
        lses.append(lse)
    wts = jax.nn.softmax(jnp.stack(lses), axis=0)
    ob = jnp.einsum('gbth,gbthd->bthd', wts, jnp.stack(outs))
    yb = ob.reshape(b, t, W_B).astype(hn.dtype) * jax.nn.silu(gb)
    y = jnp.concatenate([ya, yb], axis=-1) @ w_out
    if win_kv is None:
        keep = min(WIN_MAX, t)
        k_rows, v_rows = k[:, t - keep:], v[:, t - keep:]
    else:
        k_rows, v_rows = k, v
    return y, conv_new, h_last, k_rows, v_rows


def _diff_core(q, k, v, dist, rel_bias, lam):
    tq, tk = dist.shape
    bias = rel_bias[_t5_bucket(dist)].reshape(tq, tk, 2, H_C)
    bias = jnp.transpose(bias, (2, 3, 0, 1))
    sc = jnp.einsum('bqhcd,bkhcd->bchqk', q, k, preferred_element_type=jnp.float32) + bias
    sc = jnp.where(dist >= 0, sc, NEG)
    p = jax.nn.softmax(sc, axis=-1)
    w = p[:, 0] - lam * p[:, 1]
    return jnp.einsum('bhqk,bkhv->bqhv', w, v)


def _diff_attn_prompt(q, k, v, rel_bias, lam):
    b, s = q.shape[:2]
    nb = s // Q_BLK
    qblocks = jnp.moveaxis(q.reshape(b, nb, Q_BLK, H_C, 2, DH_C), 1, 0)
    kpos = jnp.arange(s)

    def one(args):
        qb, start = args
        qpos = start + jnp.arange(Q_BLK)
        return _diff_core(qb, k, v, qpos[:, None] - kpos[None, :], rel_bias, lam)

    o = lax.map(one, (qblocks, jnp.arange(nb) * Q_BLK))
    return jnp.moveaxis(o, 0, 1).reshape(b, s, H_C, 2 * DH_C)


def _hgrn2(q, f_logit, iv, lb, s0):
    b, t, h, dk = q.shape
    dv = iv.shape[-1]
    f = lb + (1.0 - lb) * jax.nn.sigmoid(f_logit.astype(jnp.float32))
    log_f = jnp.log(f)
    kk = 1.0 - f
    qf = jax.nn.silu(q.astype(jnp.float32)) * (dk ** -0.5)
    vv = iv.astype(jnp.float32)
    c = min(CHUNK_D, t)
    tp = -(-t // c) * c
    n = tp // c
    pad = ((0, 0), (0, tp - t), (0, 0), (0, 0))

    def chunks(z):
        return jnp.pad(z, pad).reshape(b, n, c, h, z.shape[-1])

    qf, kk, vv, log_f = chunks(qf), chunks(kk), chunks(vv), chunks(log_f)
    bc = jnp.cumsum(log_f, axis=2)
    causal = jnp.tril(jnp.ones((c, c), bool))
    diff = bc[:, :, :, None] - bc[:, :, None, :]
    decay = jnp.exp(jnp.where(causal[None, None, :, :, None, None], diff, NEG))
    att = jnp.einsum('bnthd,bnshd,bntshd->bnhts', qf, kk, decay)
    o_intra = jnp.einsum('bnhts,bnshv->bnthv', att, vv)
    b_last = bc[:, :, -1]
    u = jnp.einsum('bnshd,bnshv->bnhdv', kk * jnp.exp(b_last[:, :, None] - bc), vv)
    g = jnp.exp(b_last)

    def step(s, inp):
        g_n, u_n = inp
        return g_n[..., None] * s + u_n, s

    s_fin, s_start = lax.scan(step, s0.astype(jnp.float32), (jnp.moveaxis(g, 1, 0), jnp.moveaxis(u, 1, 0)))
    o_inter = jnp.einsum('bnthd,bnhdv->bnthv', qf * jnp.exp(bc), jnp.moveaxis(s_start, 0, 1))
    o = (o_intra + o_inter).reshape(b, tp, h, dv)[:, :t]
    return o, s_fin


def _mixer_cd(hn, w_in, lam_c, subln, lb, gnorm, w_out, rel_bias, lam_init, s0, kv_past):
    b, t, _ = hn.shape
    qc, kc, vc, gc, qd, fd, idd, gd = _split(hn @ w_in, (W_C, W_C, W_C, W_C, W_DK, W_DK, W_D, W_D))
    qc = qc.reshape(b, t, H_C, 2, DH_C) * (DH_C ** -0.5)
    kc5 = kc.reshape(b, t, H_C, 2, DH_C)
    vc4 = vc.reshape(b, t, H_C, 2 * DH_C)
    lf = lam_c.astype(jnp.float32)
    lam = jnp.exp(jnp.sum(lf[0] * lf[1])) - jnp.exp(jnp.sum(lf[2] * lf[3])) + lam_init
    if kv_past is None:
        oc = _diff_attn_prompt(qc, kc5, vc4, rel_bias, lam)
    else:
        kp, vp = kv_past
        past = kp.shape[1]
        k_all = jnp.concatenate([kp.reshape(b, past, H_C, 2, DH_C).astype(kc5.dtype), kc5], axis=1)
        v_all = jnp.concatenate([vp.astype(vc4.dtype), vc4], axis=1)
        dist = (past + jnp.arange(t))[:, None] - jnp.arange(past + t)[None, :]
        oc = _diff_core(qc, k_all, v_all, dist, rel_bias, lam)
    oc = _rmsnorm(oc, subln) * (1.0 - lam_init)
    yc = oc.reshape(b, t, W_C).astype(hn.dtype) * jax.nn.silu(gc)
    od, s_last = _hgrn2(qd.reshape(b, t, H_D, DK_D), fd.reshape(b, t, H_D, DK_D),
                        idd.reshape(b, t, H_D, DV_D), lb.reshape(H_D, DK_D), s0)
    od = _rmsnorm(od, gnorm)
    yd = od.reshape(b, t, W_D).astype(hn.dtype) * jax.nn.silu(gd)
    y = jnp.concatenate([yc, yd], axis=-1) @ w_out
    return y, kc.reshape(b, t, H_C, 2 * DH_C), vc4, s_last


def setup_inputs(seed: int = 0) -> dict:
    key = jax.random.key(seed)
    ks = iter(jax.random.split(key, 40))

    def nrm(shape, scale):
        return jax.random.normal(next(ks), shape, jnp.float32) * scale

    n_pages = PAST_LEN // PAGE_SIZE
    n_used = DEC_BATCH * n_pages
    n_phys = n_used + max(1, n_used // 4)
    l_buf = min(WIN_MAX, PAST_LEN)
    page_table = jax.random.permutation(next(ks), n_phys)[:n_used].reshape(DEC_BATCH, n_pages).astype(jnp.int32)
    u = jax.random.uniform(next(ks), (N_EVEN, D_A), jnp.float32, 0.9, 0.999)
    a_base = u ** (1.0 / RG_C)
    lam_a = jnp.log(a_base) - jnp.log1p(-a_base)
    bw = D_A // N_BLK_A
    return {
        'x_prompt': nrm((BATCH, SEQ, D_MODEL), 1.0),
        'x_sample': nrm((DEC_BATCH, DEC_SEQ, D_MODEL), 1.0),
        'state_conv_a': nrm((N_EVEN, DEC_BATCH, CONV_W - 1, D_A), 1.0),
        'state_h_a': nrm((N_EVEN, DEC_BATCH, D_A), 0.5),
        'cache_win_k': nrm((N_EVEN, DEC_BATCH, l_buf, H_B, DH_B), 1.0),
        'cache_win_v': nrm((N_EVEN, DEC_BATCH, l_buf, H_B, DH_B), 1.0),
        'cache_k_c': nrm((N_ODD, n_phys, PAGE_SIZE, H_C, 2 * DH_C), 1.0),
        'cache_v_c': nrm((N_ODD, n_phys, PAGE_SIZE, H_C, 2 * DH_C), 1.0),
        'state_s_d': nrm((N_ODD, DEC_BATCH, H_D, DK_D, DV_D), 0.3),
        'page_table': page_table,
        'norm_g': 1.0 + nrm((DEPTH, D_MODEL), 0.05),
        'norm_final': 1.0 + nrm((D_MODEL,), 0.05),
        'rel_bias': nrm((N_BUCKETS, N_BIAS), 0.3),
        'w_in_ab': nrm((N_EVEN, D_MODEL, IN_AB), D_MODEL ** -0.5),
        'conv_w_a': nrm((N_EVEN, CONV_W, D_A), CONV_W ** -0.5),
        'conv_b_a': nrm((N_EVEN, D_A), 0.02),
        'w_r_a': nrm((N_EVEN, N_BLK_A, bw, bw), bw ** -0.5),
        'b_r_a': nrm((N_EVEN, N_BLK_A, bw), 0.02),
        'w_i_a': nrm((N_EVEN, N_BLK_A, bw, bw), bw ** -0.5),
        'b_i_a': nrm((N_EVEN, N_BLK_A, bw), 0.02),
        'lam_a': lam_a,
        'w_out_ab': nrm((N_EVEN, OUT_AB, D_MODEL), OUT_AB ** -0.5),
        'w_in_cd': nrm((N_ODD, D_MODEL, IN_CD), D_MODEL ** -0.5),
        'lam_c': nrm((N_ODD, 4, DH_C), 0.1),
        'subln_c': 1.0 + nrm((N_ODD, 2 * DH_C), 0.05),
        'lb_d': 1.0 + nrm((DEPTH, W_DK), 0.1),
        'gnorm_d': 1.0 + nrm((N_ODD, DV_D), 0.05),
        'w_out_cd': nrm((N_ODD, OUT_CD, D_MODEL), OUT_CD ** -0.5),
    }


def reference(x_prompt, x_sample, state_conv_a, state_h_a, cache_win_k, cache_win_v, cache_k_c, cache_v_c,
              state_s_d, page_table, norm_g, norm_final, rel_bias, w_in_ab, conv_w_a, conv_b_a, w_r_a, b_r_a,
              w_i_a, b_i_a, lam_a, w_out_ab, w_in_cd, lam_c, subln_c, lb_d, gnorm_d, w_out_cd):
    bp = x_prompt.shape[0]
    bs = x_sample.shape[0]
    n_ctx = page_table.shape[1] * PAGE_SIZE
    lb_soft = jax.nn.softmax(lb_d.astype(jnp.float32), axis=0)
    lb_all = jnp.cumsum(lb_soft, axis=0) - lb_soft[0]
    xp, xs = x_prompt, x_sample
    conv_p, h_p, wk_p, wv_p, kc_p, vc_p, s_p = [], [], [], [], [], [], []
    conv_s, h_s, wk_s, wv_s, kc_s, vc_s, s_s = [], [], [], [], [], [], []
    for layer in range(DEPTH):
        hp = _rmsnorm(xp, norm_g[layer])
        hs = _rmsnorm(xs, norm_g[layer])
        if layer % 2 == 0:
            e = layer // 2
            w = (w_in_ab[e], conv_w_a[e], conv_b_a[e], w_r_a[e], b_r_a[e], w_i_a[e], b_i_a[e], lam_a[e],
                 w_out_ab[e], rel_bias)
            yp, c1, h1, k1, v1 = _mixer_ab(hp, *w, jnp.zeros((bp, CONV_W - 1, D_A), hp.dtype),
                                           jnp.zeros((bp, D_A), jnp.float32), None)
            ys, c2, h2, k2, v2 = _mixer_ab(hs, *w, state_conv_a[e], state_h_a[e],
                                           (cache_win_k[e], cache_win_v[e]))
            conv_p.append(c1); h_p.append(h1); wk_p.append(k1); wv_p.append(v1)
            conv_s.append(c2); h_s.append(h2); wk_s.append(k2); wv_s.append(v2)
        else:
            o = layer // 2
            lam_init = 0.8 - 0.6 * math.exp(-0.3 * layer)
            w = (w_in_cd[o], lam_c[o], subln_c[o], lb_all[layer], gnorm_d[o], w_out_cd[o], rel_bias, lam_init)
            yp, k1, v1, s1 = _mixer_cd(hp, *w, jnp.zeros((bp, H_D, DK_D, DV_D), jnp.float32), None)
            kv_past = (cache_k_c[o][page_table].reshape(bs, n_ctx, H_C, 2 * DH_C),
                       cache_v_c[o][page_table].reshape(bs, n_ctx, H_C, 2 * DH_C))
            ys, k2, v2, s2 = _mixer_cd(hs, *w, state_s_d[o], kv_past)
            kc_p.append(k1); vc_p.append(v1); s_p.append(s1)
            kc_s.append(k2); vc_s.append(v2); s_s.append(s2)
        xp = xp + yp
        xs = xs + ys
    y_prompt = _rmsnorm(xp, norm_final)
    y_sample = _rmsnorm(xs, norm_final)
    return (y_prompt, y_sample,
            jnp.stack(conv_p), jnp.stack(h_p), jnp.stack(wk_p), jnp.stack(wv_p),
            jnp.stack(kc_p), jnp.stack(vc_p), jnp.stack(s_p),
            jnp.stack(conv_s), jnp.stack(h_s), jnp.stack(wk_s), jnp.stack(wv_s),
            jnp.stack(kc_s), jnp.stack(vc_s), jnp.stack(s_s))
```

```python
import functools
import math

import numpy as np
import jax
import jax.numpy as jnp
from jax import lax
from jax.experimental import pallas as pl
from jax.experimental.pallas import tpu as pltpu

F32 = jnp.float32
BF16 = jnp.bfloat16

N_BLK_A = 8
CONV_W = 4
RG_C = 8.0
DH_B = 64
DIL_PATTERNS = ((128, 1), (512, 4), (2048, 16))
BAND = 128
WIN_MAX = 2048
DH_C = 64
DK_D = 128
DV_D = 128
CHUNK_D = 16
N_BUCKETS = 32
T5_MAX_DIST = 2048
PAGE = 128
EPS = 1e-6
NEG = -1e30
MIB = 2 ** 20


def _cparams(sem, vmem_mib):
    return pltpu.CompilerParams(dimension_semantics=sem, vmem_limit_bytes=vmem_mib * MIB)


def _dot(a, b):
    return jnp.dot(a, b, preferred_element_type=F32)


def _dot_nt(a, b):
    return lax.dot_general(a, b, (((1,), (1,)), ((), ())), preferred_element_type=F32)


def _dot_tn(a, b):
    return lax.dot_general(a, b, (((0,), (0,)), ((), ())), preferred_element_type=F32)


def _sigmoid(x):
    return 1.0 / (1.0 + jnp.exp(-x))


def _silu(x):
    return x * _sigmoid(x)


def _rms(x, g):
    return x * lax.rsqrt(jnp.mean(x * x, axis=-1, keepdims=True) + EPS) * g


def _t5_bucket(dist):
    n = jnp.maximum(dist, 0)
    exact = N_BUCKETS // 2
    nf = jnp.maximum(n, 1).astype(F32)
    large = exact + (jnp.log(nf / exact) / math.log(T5_MAX_DIST / exact) * (N_BUCKETS - exact)).astype(jnp.int32)
    large = jnp.minimum(large, N_BUCKETS - 1)
    return jnp.where(n < exact, n, large)


def _norm_proj_body(x_ref, g_ref, w_ref, *o_refs):
    hn = _rms(x_ref[...], g_ref[...]).astype(BF16)
    off = 0
    for o_ref in o_refs:
        w = o_ref.shape[-1]
        o_ref[...] = _dot(hn, w_ref[:, off:off + w])
        off += w


def _norm_proj(x, g, w, widths, tm):
    t, d = x.shape
    n = w.shape[1]
    return pl.pallas_call(
        _norm_proj_body,
        grid=(t // tm,),
        in_specs=[pl.BlockSpec((tm, d), lambda i: (i, 0)),
                  pl.BlockSpec((1, d), lambda i: (0, 0)),
                  pl.BlockSpec((d, n), lambda i: (0, 0))],
        out_specs=[pl.BlockSpec((tm, wd), lambda i: (i, 0)) for wd in widths],
        out_shape=[jax.ShapeDtypeStruct((t, wd), F32) for wd in widths],
        compiler_params=_cparams(("parallel",), 56),
        name="norm_proj",
    )(x, g.reshape(1, d), w.astype(BF16))


def _rglru_gates(xc, wr, br, wi, bi, lam):
    xcb = xc.astype(BF16)
    r = _sigmoid(_dot(xcb, wr) + br)
    gi = _sigmoid(_dot(xcb, wi) + bi)
    nl = -lam
    softplus = jnp.maximum(nl, 0.0) + jnp.log(1.0 + jnp.exp(-jnp.abs(nl)))
    log_a = (-RG_C) * r * softplus
    a = jnp.exp(log_a)
    u = jnp.sqrt(1.0 - jnp.exp(2.0 * log_a)) * (gi * xc)
    return a, u


def _rglru_prompt_body(xa_ref, ga_ref, cb_ref, h0_ref, cw_ref, cbias_ref, wr_ref, br_ref, wi_ref, bi_ref,
                       lam_ref, ya_ref, cnew_ref, hl_ref, xp_scr, h_scr, *, tb):
    j = pl.program_id(1)

    @pl.when(j == 0)
    def _():
        xp_scr[5:8, :] = cb_ref[0]
        h_scr[...] = h0_ref[0]

    xp_scr[8:8 + tb, :] = xa_ref[0]
    cw = cw_ref[...]
    xc = cbias_ref[...] + xp_scr[5:5 + tb, :] * cw[0:1, :]
    for k in range(1, CONV_W):
        xc = xc + xp_scr[5 + k:5 + k + tb, :] * cw[k:k + 1, :]
    tail = xp_scr[5 + tb:8 + tb, :]
    cnew_ref[0] = tail
    xp_scr[5:8, :] = tail

    a, u = _rglru_gates(xc, wr_ref[...], br_ref[...], wi_ref[...], bi_ref[...], lam_ref[...])
    row = lax.broadcasted_iota(jnp.int32, a.shape, 0)
    s = 1
    while s < tb:
        keep = row >= s
        u = jnp.where(keep, a * pltpu.roll(u, s, 0) + u, u)
        a = jnp.where(keep, a * pltpu.roll(a, s, 0), a)
        s *= 2
    h = a * h_scr[...] + u
    h_last = h[tb - 1:tb, :]
    h_scr[...] = h_last
    hl_ref[0] = h_last
    ya_ref[0] = h * _silu(ga_ref[0])


def _rglru_decode_body(xa_ref, ga_ref, cb_ref, h0_ref, cw_ref, cbias_ref, wr_ref, br_ref, wi_ref, bi_ref,
                       lam_ref, ya_ref, cnew_ref, hl_ref):
    nt = xa_ref.shape[0]
    xp = [cb_ref[i] for i in range(CONV_W - 1)] + [xa_ref[t] for t in range(nt)]
    cw = cw_ref[...]
    h = h0_ref[...]
    for t in range(nt):
        xc = cbias_ref[...] + xp[t] * cw[0:1, :]
        for k in range(1, CONV_W):
            xc = xc + xp[t + k] * cw[k:k + 1, :]
        a, u = _rglru_gates(xc, wr_ref[...], br_ref[...], wi_ref[...], bi_ref[...], lam_ref[...])
        h = a * h + u
        ya_ref[t] = h * _silu(ga_ref[t])
    for i in range(CONV_W - 1):
        cnew_ref[i] = xp[nt + i]
    hl_ref[...] = h


def _block_diag(w):
    g, bi, bj = w.shape
    eye = jnp.eye(g, dtype=w.dtype)
    return (w[:, :, None, :] * eye[:, None, :, None]).reshape(g * bi, g * bj)


def _gate_params(conv_w, conv_b, w_r, b_r, w_i, b_i, lam):
    c = conv_w.shape[-1]
    return (conv_w, conv_b.reshape(1, c), _block_diag(w_r).astype(BF16), b_r.reshape(1, c),
            _block_diag(w_i).astype(BF16), b_i.reshape(1, c), lam.reshape(1, c))


def _rglru_prompt(xa, ga, conv_buf, h0, gate_params, tb):
    b, t, c = xa.shape
    full = lambda shape: pl.BlockSpec(shape, lambda i, j: (0,) * len(shape))
    seq = pl.BlockSpec((1, tb, c), lambda i, j: (i, j, 0))
    return pl.pallas_call(
        functools.partial(_rglru_prompt_body, tb=tb),
        grid=(b, t // tb),
        in_specs=[seq, seq,
                  pl.BlockSpec((1, CONV_W - 1, c), lambda i, j: (i, 0, 0)),
                  pl.BlockSpec((1, 1, c), lambda i, j: (i, 0, 0)),
                  full((CONV_W, c)), full((1, c)), full((c, c)), full((1, c)), full((c, c)), full((1, c)),
                  full((1, c))],
        out_specs=[seq,
                   pl.BlockSpec((1, CONV_W - 1, c), lambda i, j: (i, 0, 0)),
                   pl.BlockSpec((1, 1, c), lambda i, j: (i, 0, 0))],
        out_shape=[jax.ShapeDtypeStruct((b, t, c), F32),
                   jax.ShapeDtypeStruct((b, CONV_W - 1, c), F32),
                   jax.ShapeDtypeStruct((b, 1, c), F32)],
        scratch_shapes=[pltpu.VMEM((tb + 8, c), F32), pltpu.VMEM((1, c), F32)],
        compiler_params=_cparams(("parallel", "arbitrary"), 40),
        name="rglru_prompt",
    )(xa, ga, conv_buf, h0.reshape(b, 1, c), *gate_params)


def _rglru_decode(xa, ga, conv_buf, h0, gate_params):
    b, t, c = xa.shape
    tm = lambda z: jnp.transpose(z, (1, 0, 2))
    ya, cnew, hl = pl.pallas_call(
        _rglru_decode_body,
        out_shape=[jax.ShapeDtypeStruct((t, b, c), F32),
                   jax.ShapeDtypeStruct((CONV_W - 1, b, c), F32),
                   jax.ShapeDtypeStruct((b, c), F32)],
        compiler_params=pltpu.CompilerParams(vmem_limit_bytes=40 * MIB),
        name="rglru_decode",
    )(tm(xa), tm(ga), tm(conv_buf), h0, *gate_params)
    return tm(ya), tm(cnew), hl


def _dil_attn_body(q_ref, kp_ref, kc_ref, vp_ref, vc_ref, bt_ref, o_ref, lse_ref, *, n_heads):
    has_prev = pl.program_id(1) > 0
    q = q_ref[0, 0] * (DH_B ** -0.5)
    kp, kc, vp, vc = kp_ref[0, 0], kc_ref[0, 0], vp_ref[0, 0], vc_ref[0, 0]
    outs, lses = [], []
    for h in range(n_heads):
        sl = slice(h * DH_B, (h + 1) * DH_B)
        qh = q[:, sl].astype(BF16)
        sp = _dot_nt(qh, kp[:, sl].astype(BF16)) + bt_ref[h, 0]
        sp = jnp.where(has_prev, sp, NEG)
        sc = _dot_nt(qh, kc[:, sl].astype(BF16)) + bt_ref[h, 1]
        m = jnp.maximum(jnp.max(sp, axis=-1, keepdims=True), jnp.max(sc, axis=-1, keepdims=True))
        pp = jnp.exp(sp - m)
        pc = jnp.exp(sc - m)
        l = jnp.sum(pp, axis=-1, keepdims=True) + jnp.sum(pc, axis=-1, keepdims=True)
        o = (_dot(pp.astype(BF16), vp[:, sl].astype(BF16)) + _dot(pc.astype(BF16), vc[:, sl].astype(BF16))) / l
        outs.append(o)
        lses.append(jnp.broadcast_to(m + jnp.log(l), o.shape))
    o_ref[0, 0] = jnp.concatenate(outs, axis=-1)
    lse_ref[0, 0] = jnp.concatenate(lses, axis=-1)


def _band_bias_tiles(bias_j):
    i = np.arange(BAND)[:, None]
    j = np.arange(BAND)[None, :]
    prev = jnp.where((j >= i)[..., None], bias_j[np.clip(i + BAND - j, 0, BAND)], NEG)
    cur = jnp.where((j <= i)[..., None], bias_j[np.clip(i - j, 0, BAND)], NEG)
    return jnp.transpose(jnp.stack([prev, cur]), (3, 0, 1, 2))


def _dil_attn(q, k, v, bias_tiles, dil):
    b, s, w = q.shape
    nb = s // (dil * BAND)
    n_heads = w // DH_B
    r4 = lambda z: z.reshape(b, nb, BAND, dil * w)
    cur = pl.BlockSpec((1, 1, BAND, w), lambda i, n, r: (i, n, 0, r))
    prev = pl.BlockSpec((1, 1, BAND, w), lambda i, n, r: (i, jnp.maximum(n - 1, 0), 0, r))
    o, lse = pl.pallas_call(
        functools.partial(_dil_attn_body, n_heads=n_heads),
        grid=(b, nb, dil),
        in_specs=[cur, prev, cur, prev, cur,
                  pl.BlockSpec((n_heads, 2, BAND, BAND), lambda i, n, r: (0, 0, 0, 0))],
        out_specs=[cur, cur],
        out_shape=[jax.ShapeDtypeStruct((b, nb, BAND, dil * w), F32)] * 2,
        compiler_params=_cparams(("parallel", "parallel", "parallel"), 32),
        name=f"dil_attn_{dil}",
    )(r4(q), r4(k), r4(k), r4(v), r4(v), bias_tiles)
    return o.reshape(b, s, w), lse.reshape(b, s, w)


def _out_proj_body(*refs, merge, final_norm):
    x_ref, ya_ref = refs[0], refs[1]
    pos = 2
    if merge:
        o1, o2, o3, l1, l2, l3, gb_ref = refs[pos:pos + 7]
        pos += 7
        lse1, lse2, lse3 = l1[...], l2[...], l3[...]
        m = jnp.maximum(jnp.maximum(lse1, lse2), lse3)
        e1, e2, e3 = jnp.exp(lse1 - m), jnp.exp(lse2 - m), jnp.exp(lse3 - m)
        ob = (e1 * o1[...] + e2 * o2[...] + e3 * o3[...]) / (e1 + e2 + e3)
        yb = ob * _silu(gb_ref[...])
    else:
        yb = refs[pos][...]
        pos += 1
    wa_ref, wb_ref = refs[pos], refs[pos + 1]
    pos += 2
    y = x_ref[...] + _dot(ya_ref[...].astype(BF16), wa_ref[...]) + _dot(yb.astype(BF16), wb_ref[...])
    if final_norm:
        y = _rms(y, refs[pos][...])
        pos += 1
    refs[pos][...] = y


def _out_proj(x, ya, yb_parts, w_out, final_g, tm):
    t, d = x.shape
    ca = ya.shape[1]
    merge = len(yb_parts) > 1
    cb = yb_parts[0].shape[1]
    row = lambda c: pl.BlockSpec((tm, c), lambda i: (i, 0))
    const = lambda shape: pl.BlockSpec(shape, lambda i: (0, 0))
    wa = w_out[:ca].astype(BF16)
    wb = w_out[ca:].astype(BF16)
    args = [x, ya, *yb_parts, wa, wb]
    in_specs = [row(d), row(ca)] + [row(cb)] * len(yb_parts) + [const((ca, d)), const((cb, d))]
    if final_g is not None:
        args.append(final_g.reshape(1, d))
        in_specs.append(const((1, d)))
    return pl.pallas_call(
        functools.partial(_out_proj_body, merge=merge, final_norm=final_g is not None),
        grid=(t // tm,),
        in_specs=in_specs,
        out_specs=row(d),
        out_shape=jax.ShapeDtypeStruct((t, d), F32),
        compiler_params=_cparams(("parallel",), 48),
        name="out_proj",
    )(*args)


def _lam_from(lamc, lam_init):
    s01 = jnp.sum(lamc[0:1, :] * lamc[1:2, :], axis=-1, keepdims=True)
    s23 = jnp.sum(lamc[2:3, :] * lamc[3:4, :], axis=-1, keepdims=True)
    return jnp.exp(s01) - jnp.exp(s23) + lam_init


def _diff_attn_body(q_ref, k_ref, v_ref, g_ref, bt_ref, subln_ref, lamc_ref, o_ref, *, tq, n_const, lam_init):
    qi = pl.program_id(2)
    q = q_ref[0] * (DH_C ** -0.5)
    lane = lax.broadcasted_iota(jnp.int32, q.shape, 1)
    q0 = jnp.where(lane < DH_C, q, 0.0).astype(BF16)
    q1 = jnp.where(lane >= DH_C, q, 0.0).astype(BF16)
    dv = v_ref.shape[-1]

    def update(s, carry, v):
        m, l, acc = carry
        m_new = jnp.maximum(m, jnp.max(s, axis=-1, keepdims=True))
        alpha = jnp.exp(m - m_new)
        p = jnp.exp(s - m_new)
        l = alpha * l + jnp.sum(p, axis=-1, keepdims=True)
        acc = alpha * acc + _dot(p.astype(BF16), v)
        return m_new, l, acc

    def body(kj, carry):
        c0, c1 = carry
        start = pl.multiple_of(kj * tq, tq)
        k = k_ref[0, pl.ds(start, tq), :].astype(BF16)
        v = v_ref[0, pl.ds(start, tq), :].astype(BF16)
        d = jnp.minimum(qi - kj, n_const)
        c0 = update(_dot_nt(q0, k) + bt_ref[0, d], c0, v)
        c1 = update(_dot_nt(q1, k) + bt_ref[1, d], c1, v)
        return c0, c1

    init = (jnp.full((tq, 1), NEG, F32), jnp.zeros((tq, 1), F32), jnp.zeros((tq, dv), F32))
    (_, l0, a0), (_, l1, a1) = lax.fori_loop(0, qi + 1, body, (init, init))
    lam = _lam_from(lamc_ref[...], lam_init)
    o = a0 / l0 - lam * (a1 / l1)
    oc = _rms(o, subln_ref[...]) * (1.0 - lam_init)
    o_ref[0] = oc * _silu(g_ref[0])


def _diff_bias_tiles(bias_dist, tq, n_const, n_heads):
    i = np.arange(tq)[:, None]
    j = np.arange(tq)[None, :]
    tiles = []
    for d in range(n_const + 1):
        dist = d * tq + i - j
        tiles.append(jnp.where((dist >= 0)[..., None], bias_dist[np.clip(dist, 0, None)], NEG))
    t = jnp.transpose(jnp.stack(tiles), (3, 0, 1, 2))
    return t.reshape(2, n_heads, n_const + 1, tq, tq)


def _const_tile_index(tq):
    exact = N_BUCKETS // 2
    sat = math.ceil(exact * (T5_MAX_DIST / exact) ** ((N_BUCKETS - 1 - exact) / (N_BUCKETS - exact))) + 1
    return -(-(sat + tq - 1) // tq)


def _diff_attn_prompt(q, k, v, g, rel_bias, subln, lamc, lam_init, tq):
    b, s, w = q.shape
    hw = 2 * DH_C
    n_heads = w // hw
    n_const = _const_tile_index(tq)
    n_dist = n_const * tq + tq
    bias_dist = rel_bias[_t5_bucket(jnp.arange(n_dist))]
    tiles = _diff_bias_tiles(bias_dist, tq, n_const, n_heads)
    qspec = pl.BlockSpec((1, tq, hw), lambda i, h, j: (i, j, h))
    kvspec = pl.BlockSpec((1, s, hw), lambda i, h, j: (i, 0, h))
    return pl.pallas_call(
        functools.partial(_diff_attn_body, tq=tq, n_const=n_const, lam_init=lam_init),
        grid=(b, n_heads, s // tq),
        in_specs=[qspec, kvspec, kvspec, qspec,
                  pl.BlockSpec((2, None, n_const + 1, tq, tq), lambda i, h, j: (0, h, 0, 0, 0)),
                  pl.BlockSpec((1, hw), lambda i, h, j: (0, 0)),
                  pl.BlockSpec((4, DH_C), lambda i, h, j: (0, 0))],
        out_specs=qspec,
        out_shape=jax.ShapeDtypeStruct((b, s, w), F32),
        compiler_params=_cparams(("parallel", "parallel", "arbitrary"), 48),
        name="diff_attn_prompt",
    )(q, k, v, g, tiles, subln.reshape(1, hw), lamc)


def _hgrn2_body(q_ref, f_ref, v_ref, g_ref, lb_ref, gn_ref, s0_ref, y_ref, sl_ref, st_scr, o_scr, *, tb, chunk):
    j = pl.program_id(2)

    @pl.when(j == 0)
    def _():
        st_scr[...] = s0_ref[0, 0].T

    lb = lb_ref[...]
    f = lb + (1.0 - lb) * _sigmoid(f_ref[0])
    log_f = jnp.log(f)
    kk = 1.0 - f
    qf = _silu(q_ref[0]) * (DK_D ** -0.5)
    v = v_ref[0].astype(BF16)

    row = lax.broadcasted_iota(jnp.int32, log_f.shape, 0)
    pos = row % chunk
    bc = log_f
    tot = log_f
    s = 1
    while s < chunk:
        bc = jnp.where(pos >= s, bc + pltpu.roll(bc, s, 0), bc)
        tot = tot + jnp.where((pos & s) == 0, pltpu.roll(tot, tb - s, 0), pltpu.roll(tot, s, 0))
        s *= 2
    qd = (qf * jnp.exp(bc)).astype(BF16)
    kd = (kk * jnp.exp(-bc)).astype(BF16)
    kl = (kk * jnp.exp(tot - bc)).astype(BF16)
    gdec = jnp.exp(tot)

    att = _dot_nt(qd, kd)
    r2 = lax.broadcasted_iota(jnp.int32, att.shape, 0)
    c2 = lax.broadcasted_iota(jnp.int32, att.shape, 1)
    att = jnp.where((r2 // chunk == c2 // chunk) & (c2 <= r2), att, 0.0)
    o_scr[...] = _dot(att.astype(BF16), v)

    st = st_scr[...]
    for n in range(tb // chunk):
        sl = slice(n * chunk, (n + 1) * chunk)
        o_scr[sl, :] += _dot_nt(qd[sl], st.astype(BF16))
        st = st * gdec[n * chunk:n * chunk + 1, :] + _dot_tn(v[sl], kl[sl])
    st_scr[...] = st

    @pl.when(j == pl.num_programs(2) - 1)
    def _():
        sl_ref[0, 0] = st.T

    y_ref[0] = _rms(o_scr[...], gn_ref[...]) * _silu(g_ref[0])


def _hgrn2_prompt(q, f, v, g, lb, gnorm, s0, tb):
    b, s, w = q.shape
    n_heads = w // DK_D
    seq = pl.BlockSpec((1, tb, DK_D), lambda i, h, j: (i, j, h))
    st = pl.BlockSpec((1, 1, DK_D, DV_D), lambda i, h, j: (i, h, 0, 0))
    return pl.pallas_call(
        functools.partial(_hgrn2_body, tb=tb, chunk=CHUNK_D),
        grid=(b, n_heads, s // tb),
        in_specs=[seq, seq, seq, seq,
                  pl.BlockSpec((1, DK_D), lambda i, h, j: (0, h)),
                  pl.BlockSpec((1, DV_D), lambda i, h, j: (0, 0)),
                  st],
        out_specs=[seq, st],
        out_shape=[jax.ShapeDtypeStruct((b, s, w), F32),
                   jax.ShapeDtypeStruct((b, n_heads, DK_D, DV_D), F32)],
        scratch_shapes=[pltpu.VMEM((DV_D, DK_D), F32), pltpu.VMEM((tb, DV_D), F32)],
        compiler_params=_cparams(("parallel", "parallel", "arbitrary"), 32),
        name="hgrn2_prompt",
    )(q, f, v, g, lb.reshape(1, w), gnorm.reshape(1, DV_D), s0)


def _dil_decode_body(qe_ref, kn_ref, vn_ref, gb_ref, kt_ref, ks_ref, vt_ref, vs_ref, b1_ref, b2_ref, b3_ref,
                     bn_ref, y_ref, *, nt, n_heads, w, tail1):
    qe = qe_ref[0].astype(BF16)
    kt = kt_ref[0].astype(BF16)
    vt = vt_ref[0].astype(BF16)
    ks = ks_ref[0].astype(BF16)
    vs = vs_ref[0].astype(BF16)
    kn = kn_ref[0].astype(BF16)
    vn = vn_ref[0].astype(BF16)
    rows = qe.shape[0]
    t_of_row = lax.broadcasted_iota(jnp.int32, (rows, w), 0) % nt

    s1 = _dot_nt(qe, kt[tail1:]) + b1_ref[...]
    s2 = _dot_nt(qe, kt) + b2_ref[...]
    qe3 = jnp.concatenate([jnp.where(t_of_row == t, qe, jnp.zeros_like(qe)) for t in range(nt)], axis=-1)
    s3 = _dot_nt(qe3, ks) + b3_ref[...]
    sn = _dot_nt(qe, kn)
    sns = [sn + bn_ref[g] for g in range(3)]
    parts = [s1, s2, s3] + sns
    m = parts[0].max(axis=-1, keepdims=True)
    for p in parts[1:]:
        m = jnp.maximum(m, p.max(axis=-1, keepdims=True))
    ps = [jnp.exp(p - m) for p in parts]
    l = ps[0].sum(axis=-1, keepdims=True)
    for p in ps[1:]:
        l = l + p.sum(axis=-1, keepdims=True)
    pn = (ps[3] + ps[4] + ps[5]).astype(BF16)
    o = _dot(ps[0].astype(BF16), vt[tail1:]) + _dot(ps[1].astype(BF16), vt) + _dot(pn, vn)
    o3 = _dot(ps[2].astype(BF16), vs)
    for t in range(nt):
        o = o + jnp.where(t_of_row == t, o3[:, t * w:(t + 1) * w], 0.0)
    o = o / l
    lane_head = lax.broadcasted_iota(jnp.int32, (rows, w), 1) // DH_B
    row_head = lax.broadcasted_iota(jnp.int32, (rows, w), 0) // nt
    om = jnp.where(lane_head == row_head, o, 0.0)
    ob = om[0:nt]
    for h in range(1, n_heads):
        ob = ob + om[h * nt:(h + 1) * nt]
    y_ref[0] = ob * _silu(gb_ref[0])


def _dil_decode(q, k, v, gb, win_k, win_v, rel_bias):
    b, nt, w = q.shape
    l_buf = win_k.shape[1]
    n_heads = w // DH_B
    rows = n_heads * nt
    (_, d2), (_, d3) = DIL_PATTERNS[1], DIL_PATTERNS[2]
    assert l_buf == BAND * d3 and nt <= d2 and DIL_PATTERNS[0][1] == 1
    n_tail = BAND * d2
    tail1 = n_tail - BAND

    bias_j = [rel_bias[_t5_bucket(dil * jnp.arange(BAND + 1))] for _, dil in DIL_PATTERNS]
    hh = np.arange(rows)[:, None] // nt
    tt = np.arange(rows)[:, None] % nt
    i1 = np.arange(BAND)[None, :]
    j1 = BAND + tt - i1
    b1 = jnp.where(j1 <= BAND, bias_j[0][np.clip(j1, 0, BAND), hh], NEG)
    i2 = np.arange(n_tail)[None, :]
    j2 = (n_tail + tt - i2) // d2
    ok2 = ((n_tail + tt - i2) % d2 == 0) & (j2 >= 1) & (j2 <= BAND)
    b2 = jnp.where(ok2, bias_j[1][np.clip(j2, 0, BAND), hh], NEG)
    b3 = bias_j[2][BAND - np.arange(BAND)[None, :], hh]
    tn = np.arange(8)[None, :]
    jn = tt - tn
    bn = jnp.stack([jnp.where((jn >= 0) & (tn < nt), bias_j[0][np.clip(jn, 0, BAND), hh], NEG),
                    jnp.where(jn == 0, bias_j[1][0, hh], NEG),
                    jnp.where(jn == 0, bias_j[2][0, hh], NEG)])

    head_mask = (np.arange(w)[None, :] // DH_B == np.arange(n_heads)[:, None]).astype(np.float32)
    qe = (q[:, None, :, :] * (DH_B ** -0.5)) * head_mask[None, :, None, :]
    qe = qe.reshape(b, rows, w)
    pad8 = lambda z: jnp.pad(z, ((0, 0), (0, 8 - nt), (0, 0)))
    strided = lambda z: z.reshape(b, BAND, d3 * w)

    per_b = lambda r, c: pl.BlockSpec((1, r, c), lambda i: (i, 0, 0))
    tail = pl.BlockSpec((1, n_tail, w), lambda i: (i, l_buf // n_tail - 1, 0))
    const = lambda *shape: pl.BlockSpec(shape, lambda i: (0,) * len(shape))
    return pl.pallas_call(
        functools.partial(_dil_decode_body, nt=nt, n_heads=n_heads, w=w, tail1=tail1),
        grid=(b,),
        in_specs=[per_b(rows, w), per_b(8, w), per_b(8, w), per_b(nt, w),
                  tail, per_b(BAND, nt * w), tail, per_b(BAND, nt * w),
                  const(rows, BAND), const(rows, n_tail), const(rows, BAND), const(3, rows, 8)],
        out_specs=per_b(nt, w),
        out_shape=jax.ShapeDtypeStruct((b, nt, w), F32),
        compiler_params=_cparams(("parallel",), 40),
        name="dil_decode",
    )(qe, pad8(k), pad8(v), gb, win_k, strided(win_k), win_v, strided(win_v), b1, b2, b3, bn)


def _diff_decode_body(pt_ref, qe_ref, kn_ref, vn_ref, g_ref, bp_ref, bnew_ref, subln_ref, lamc_ref, *rest,
                      nt, n_heads, n_pages, lam_init):
    del pt_ref
    k_pages, v_pages, y_ref = rest[:n_pages], rest[n_pages:2 * n_pages], rest[2 * n_pages]
    qe = qe_ref[0].astype(BF16)
    half = n_heads * nt
    hw = 2 * DH_C
    ss = [_dot_nt(qe, k_pages[p][0].astype(BF16)) + bp_ref[p] for p in range(n_pages)]
    sn = _dot_nt(qe, kn_ref[0].astype(BF16)) + bnew_ref[...]
    m = sn.max(axis=-1, keepdims=True)
    for s in ss:
        m = jnp.maximum(m, s.max(axis=-1, keepdims=True))
    ps = [jnp.exp(s - m) for s in ss]
    pn = jnp.exp(sn - m)
    l = pn.sum(axis=-1, keepdims=True)
    for p in ps:
        l = l + p.sum(axis=-1, keepdims=True)
    inv = 1.0 / l
    lam = _lam_from(lamc_ref[...], lam_init)
    mix = lambda p: (p[:half] * inv[:half] - lam * (p[half:] * inv[half:])).astype(BF16)
    o = _dot(mix(pn), vn_ref[0].astype(BF16))
    for p in range(n_pages):
        o = o + _dot(mix(ps[p]), v_pages[p][0].astype(BF16))
    w = o.shape[-1]
    lane_head = lax.broadcasted_iota(jnp.int32, (half, w), 1) // hw
    row_head = lax.broadcasted_iota(jnp.int32, (half, w), 0) // nt
    om = jnp.where(lane_head == row_head, o, 0.0)
    oc = om[0:nt]
    for h in range(1, n_heads):
        oc = oc + om[h * nt:(h + 1) * nt]
    subln = subln_ref[...]
    oc = jnp.concatenate([_rms(oc[:, h * hw:(h + 1) * hw], subln) for h in range(n_heads)], axis=-1)
    y_ref[0] = oc * (1.0 - lam_init) * _silu(g_ref[0])


def _diff_decode(q, k, v, g, cache_k, cache_v, page_table, rel_bias, subln, lamc, lam_init):
    b, nt, w = q.shape
    hw = 2 * DH_C
    n_heads = w // hw
    n_pages = page_table.shape[1]
    past = n_pages * PAGE
    half = n_heads * nt
    rows = 2 * half
    cc = np.arange(rows)[:, None] // half
    hh = (np.arange(rows)[:, None] % half) // nt
    tt = np.arange(rows)[:, None] % nt
    col = cc * n_heads + hh
    bias_dist = rel_bias[_t5_bucket(jnp.arange(past + nt))]
    bp = bias_dist[past + tt - np.arange(past)[None, :], col]
    bp = jnp.transpose(bp.reshape(rows, n_pages, PAGE), (1, 0, 2))
    tn = np.arange(8)[None, :]
    bnew = jnp.where((tn <= tt) & (tn < nt), bias_dist[np.clip(tt - tn, 0, None), col], NEG)

    lane_mask = (np.arange(w)[None, :] // DH_C ==
                 (2 * hh + cc)).astype(np.float32)
    qrows = jnp.tile(q, (1, 2 * n_heads, 1))
    qe = qrows * (DH_C ** -0.5) * lane_mask[None]
    pad8 = lambda z: jnp.pad(z, ((0, 0), (0, 8 - nt), (0, 0)))

    per_b = lambda r, c: pl.BlockSpec((1, r, c), lambda i, pt: (i, 0, 0))
    const = lambda *shape: pl.BlockSpec(shape, lambda i, pt: (0,) * len(shape))
    page = lambda p: pl.BlockSpec((1, PAGE, w), lambda i, pt, p=p: (pt[i * n_pages + p], 0, 0))
    grid_spec = pltpu.PrefetchScalarGridSpec(
        num_scalar_prefetch=1,
        grid=(b,),
        in_specs=[per_b(rows, w), per_b(8, w), per_b(8, w), per_b(nt, w),
                  const(n_pages, rows, PAGE), const(rows, 8), const(1, hw), const(4, DH_C)]
                 + [page(p) for p in range(n_pages)] * 2,
        out_specs=per_b(nt, w),
    )
    return pl.pallas_call(
        functools.partial(_diff_decode_body, nt=nt, n_heads=n_heads, n_pages=n_pages, lam_init=lam_init),
        grid_spec=grid_spec,
        out_shape=jax.ShapeDtypeStruct((b, nt, w), F32),
        compiler_params=_cparams(("arbitrary",), 40),
        name="diff_decode",
    )(page_table.reshape(-1), qe, pad8(k), pad8(v), g, bp, bnew, subln.reshape(1, hw), lamc,
      *([cache_k] * n_pages), *([cache_v] * n_pages))


def _hgrn2_decode_body(qt_ref, ft_ref, v_ref, g_ref, lb_ref, gn_ref, s0_ref, y_ref, sl_ref, *, nt, n_heads):
    gn = gn_ref[...]
    for h in range(n_heads):
        lb = lb_ref[h]
        f = lb + (1.0 - lb) * _sigmoid(ft_ref[0, h])
        kk = 1.0 - f
        qf = _silu(qt_ref[0, h]) * (DK_D ** -0.5)
        st = s0_ref[0, h]
        lanes = slice(h * DV_D, (h + 1) * DV_D)
        for t in range(nt):
            st = f[:, t:t + 1] * st + kk[:, t:t + 1] * v_ref[0, t:t + 1, lanes]
            o = jnp.sum(st * qf[:, t:t + 1], axis=0, keepdims=True)
            y_ref[0, t:t + 1, lanes] = _rms(o, gn) * _silu(g_ref[0, t:t + 1, lanes])
        sl_ref[0, h] = st


def _hgrn2_decode(q, f, v, g, lb, gnorm, s0):
    b, nt, w = q.shape
    n_heads = w // DK_D
    cols = lambda z: jnp.transpose(z.reshape(b, nt, n_heads, DK_D), (0, 2, 3, 1))
    per_b = lambda *shape: pl.BlockSpec((1,) + shape, lambda i: (i,) + (0,) * len(shape))
    const = lambda *shape: pl.BlockSpec(shape, lambda i: (0,) * len(shape))
    return pl.pallas_call(
        functools.partial(_hgrn2_decode_body, nt=nt, n_heads=n_heads),
        grid=(b,),
        in_specs=[per_b(n_heads, DK_D, nt), per_b(n_heads, DK_D, nt), per_b(nt, w), per_b(nt, w),
                  const(n_heads, DK_D, 1), const(1, DV_D), per_b(n_heads, DK_D, DV_D)],
        out_specs=[per_b(nt, w), per_b(n_heads, DK_D, DV_D)],
        out_shape=[jax.ShapeDtypeStruct((b, nt, w), F32),
                   jax.ShapeDtypeStruct((b, n_heads, DK_D, DV_D), F32)],
        compiler_params=_cparams(("parallel",), 32),
        name="hgrn2_decode",
    )(cols(q), cols(f), v, g, lb.reshape(n_heads, DK_D, 1), gnorm.reshape(1, DV_D), s0)


def _layer_ab(x, group, w_in, gate_params, w_out, rel_bias, state):
    b, t, d = x.shape
    c = w_out.shape[0] // 2
    xa, ga, q, k, v, gb = [z.reshape(b, t, c) for z in _norm_proj(x.reshape(b * t, d), *group, w_in, (c,) * 6,
                                                                    min(512, b * t))]
    if state is None:
        ya, conv_new, h_last = _rglru_prompt(xa, ga, jnp.zeros((b, CONV_W - 1, c), F32), jnp.zeros((b, c), F32),
                                             gate_params, tb=256)
        h_last = h_last.reshape(b, c)
        parts = []
        for _, dil in DIL_PATTERNS:
            bias_j = rel_bias[_t5_bucket(dil * jnp.arange(BAND + 1))]
            parts.append(_dil_attn(q, k, v, _band_bias_tiles(bias_j), dil))
        yb_parts = [p[0].reshape(b * t, c) for p in parts] + [p[1].reshape(b * t, c) for p in parts]
        yb_parts.append(gb.reshape(b * t, c))
        keep = min(WIN_MAX, t)
        k_rows, v_rows = k[:, t - keep:], v[:, t - keep:]
    else:
        conv_buf, h0, win_k, win_v = state
        ya, conv_new, h_last = _rglru_decode(xa, ga, conv_buf, h0, gate_params)
        l_buf = win_k.shape[1]
        yb = _dil_decode(q, k, v, gb, win_k.reshape(b, l_buf, c), win_v.reshape(b, l_buf, c), rel_bias)
        yb_parts = [yb.reshape(b * t, c)]
        k_rows, v_rows = k, v
    x = _out_proj(x.reshape(b * t, d), ya.reshape(b * t, c), yb_parts, w_out, None, min(512, b * t))
    heads = c // DH_B
    return (x.reshape(b, t, d), conv_new, h_last, k_rows.reshape(b, -1, heads, DH_B),
            v_rows.reshape(b, -1, heads, DH_B))


def _layer_cd(x, group, w_in, lamc, subln, lb, gnorm, w_out, rel_bias, lam_init, final_g, state):
    b, t, d = x.shape
    c = w_out.shape[0] // 2
    qc, kc, vc, gc, qd, fd, idd, gd = [z.reshape(b, t, c) for z in
                                       _norm_proj(x.reshape(b * t, d), *group, w_in, (c,) * 8, min(512, b * t))]
    if state is None:
        yc = _diff_attn_prompt(qc, kc, vc, gc, rel_bias, subln, lamc, lam_init, tq=256)
        yd, s_last = _hgrn2_prompt(qd, fd, idd, gd, lb, gnorm, jnp.zeros((b, c // DK_D, DK_D, DV_D), F32), tb=256)
    else:
        s0, cache_k, cache_v, page_table = state
        n_phys = cache_k.shape[0]
        yc = _diff_decode(qc, kc, vc, gc, cache_k.reshape(n_phys, PAGE, c), cache_v.reshape(n_phys, PAGE, c),
                          page_table, rel_bias, subln, lamc, lam_init)
        yd, s_last = _hgrn2_decode(qd, fd, idd, gd, lb, gnorm, s0)
    y = _out_proj(x.reshape(b * t, d), yc.reshape(b * t, c), [yd.reshape(b * t, c)], w_out, final_g,
                  min(512, b * t))
    heads = c // (2 * DH_C)
    return (y.reshape(b, t, d), kc.reshape(b, t, heads, 2 * DH_C), vc.reshape(b, t, heads, 2 * DH_C), s_last)


def kernel(x_prompt, x_sample, state_conv_a, state_h_a, cache_win_k, cache_win_v, cache_k_c, cache_v_c, state_s_d,
           page_table, norm_g, norm_final, rel_bias, w_in_ab, conv_w_a, conv_b_a, w_r_a, b_r_a, w_i_a, b_i_a, lam_a,
           w_out_ab, w_in_cd, lam_c, subln_c, lb_d, gnorm_d, w_out_cd):
    depth = norm_g.shape[0]
    assert depth == 2
    lb_soft = jax.nn.softmax(lb_d.astype(F32), axis=0)
    lb_all = jnp.cumsum(lb_soft, axis=0) - lb_soft[0]

    gate_params = _gate_params(conv_w_a[0], conv_b_a[0], w_r_a[0], b_r_a[0], w_i_a[0], b_i_a[0], lam_a[0])
    xp, conv_p, h_p, wk_p, wv_p = _layer_ab(x_prompt, (norm_g[0],), w_in_ab[0], gate_params, w_out_ab[0], rel_bias,
                                            None)
    xs, conv_s, h_s, wk_s, wv_s = _layer_ab(x_sample, (norm_g[0],), w_in_ab[0], gate_params, w_out_ab[0], rel_bias,
                                            (state_conv_a[0], state_h_a[0], cache_win_k[0], cache_win_v[0]))
    lam_init = 0.8 - 0.6 * math.exp(-0.3 * 1)
    cd = (w_in_cd[0], lam_c[0], subln_c[0], lb_all[1], gnorm_d[0], w_out_cd[0], rel_bias, lam_init, norm_final)
    yp, kc_p, vc_p, s_p = _layer_cd(xp, (norm_g[1],), *cd, None)
    ys, kc_s, vc_s, s_s = _layer_cd(xs, (norm_g[1],), *cd, (state_s_d[0], cache_k_c[0], cache_v_c[0], page_table))
    e = lambda z: z[None]
    return (yp, ys, e(conv_p), e(h_p), e(wk_p), e(wv_p), e(kc_p), e(vc_p), e(s_p),
            e(conv_s), e(h_s), e(wk_s), e(wv_s), e(kc_s), e(vc_s), e(s_s))
```

```python
import functools
import math

import numpy as np
import jax
import jax.numpy as jnp
from jax import lax
from jax.experimental import pallas as pl
from jax.experimental.pallas import tpu as pltpu

F32 = jnp.float32
BF16 = jnp.bfloat16

N_BLK_A = 8
CONV_W = 4
RG_C = 8.0
DH_B = 64
DIL_PATTERNS = ((128, 1), (512, 4), (2048, 16))
BAND = 128
WIN_MAX = 2048
DH_C = 64
DK_D = 128
DV_D = 128
CHUNK_D = 16
N_BUCKETS = 32
T5_MAX_DIST = 2048
PAGE = 128
EPS = 1e-6
NEG = -1e30
MIB = 2 ** 20


def _cparams(sem, vmem_mib):
    return pltpu.CompilerParams(dimension_semantics=sem, vmem_limit_bytes=vmem_mib * MIB)


def _dot(a, b):
    return jnp.dot(a, b, preferred_element_type=F32)


def _dot_nt(a, b):
    return lax.dot_general(a, b, (((1,), (1,)), ((), ())), preferred_element_type=F32)


def _dot_tn(a, b):
    return lax.dot_general(a, b, (((0,), (0,)), ((), ())), preferred_element_type=F32)


def _sigmoid(x):
    return 1.0 / (1.0 + jnp.exp(-x))


def _silu(x):
    return x * _sigmoid(x)


def _rms(x, g):
    return x * lax.rsqrt(jnp.mean(x * x, axis=-1, keepdims=True) + EPS) * g


def _t5_bucket(dist):
    n = jnp.maximum(dist, 0)
    exact = N_BUCKETS // 2
    nf = jnp.maximum(n, 1).astype(F32)
    large = exact + (jnp.log(nf / exact) / math.log(T5_MAX_DIST / exact) * (N_BUCKETS - exact)).astype(jnp.int32)
    large = jnp.minimum(large, N_BUCKETS - 1)
    return jnp.where(n < exact, n, large)


def _norm_proj_body(x_ref, g_ref, w_ref, *o_refs):
    hn = _rms(x_ref[...], g_ref[...]).astype(BF16)
    off = 0
    for o_ref in o_refs:
        w = o_ref.shape[-1]
        o_ref[...] = _dot(hn, w_ref[:, off:off + w])
        off += w


def _norm_proj(x, g, w, widths, tm):
    t, d = x.shape
    n = w.shape[1]
    return pl.pallas_call(
        _norm_proj_body,
        grid=(t // tm,),
        in_specs=[pl.BlockSpec((tm, d), lambda i: (i, 0)),
                  pl.BlockSpec((1, d), lambda i: (0, 0)),
                  pl.BlockSpec((d, n), lambda i: (0, 0))],
        out_specs=[pl.BlockSpec((tm, wd), lambda i: (i, 0)) for wd in widths],
        out_shape=[jax.ShapeDtypeStruct((t, wd), F32) for wd in widths],
        compiler_params=_cparams(("parallel",), 56),
        name="norm_proj",
    )(x, g.reshape(1, d), w.astype(BF16))


def _rglru_gates(xc, wr, br, wi, bi, lam):
    xcb = xc.astype(BF16)
    r = _sigmoid(_dot(xcb, wr) + br)
    gi = _sigmoid(_dot(xcb, wi) + bi)
    nl = -lam
    softplus = jnp.maximum(nl, 0.0) + jnp.log(1.0 + jnp.exp(-jnp.abs(nl)))
    log_a = (-RG_C) * r * softplus
    a = jnp.exp(log_a)
    u = jnp.sqrt(1.0 - jnp.exp(2.0 * log_a)) * (gi * xc)
    return a, u


def _rglru_prompt_body(xa_ref, ga_ref, cb_ref, h0_ref, cw_ref, cbias_ref, wr_ref, br_ref, wi_ref, bi_ref,
                       lam_ref, ya_ref, cnew_ref, hl_ref, xp_scr, h_scr, *, tb):
    j = pl.program_id(1)

    @pl.when(j == 0)
    def _():
        xp_scr[5:8, :] = cb_ref[0]
        h_scr[...] = h0_ref[0]

    xp_scr[8:8 + tb, :] = xa_ref[0]
    cw = cw_ref[...]
    xc = cbias_ref[...] + xp_scr[5:5 + tb, :] * cw[0:1, :]
    for k in range(1, CONV_W):
        xc = xc + xp_scr[5 + k:5 + k + tb, :] * cw[k:k + 1, :]
    tail = xp_scr[5 + tb:8 + tb, :]
    cnew_ref[0] = tail
    xp_scr[5:8, :] = tail

    a, u = _rglru_gates(xc, wr_ref[...], br_ref[...], wi_ref[...], bi_ref[...], lam_ref[...])
    row = lax.broadcasted_iota(jnp.int32, a.shape, 0)
    s = 1
    while s < tb:
        keep = row >= s
        u = jnp.where(keep, a * pltpu.roll(u, s, 0) + u, u)
        a = jnp.where(keep, a * pltpu.roll(a, s, 0), a)
        s *= 2
    h = a * h_scr[...] + u
    h_last = h[tb - 1:tb, :]
    h_scr[...] = h_last
    hl_ref[0] = h_last
    ya_ref[0] = h * _silu(ga_ref[0])


def _rglru_decode_body(xa_ref, ga_ref, cb_ref, h0_ref, cw_ref, cbias_ref, wr_ref, br_ref, wi_ref, bi_ref,
                       lam_ref, ya_ref, cnew_ref, hl_ref):
    nt = xa_ref.shape[0]
    xp = [cb_ref[i] for i in range(CONV_W - 1)] + [xa_ref[t] for t in range(nt)]
    cw = cw_ref[...]
    h = h0_ref[...]
    for t in range(nt):
        xc = cbias_ref[...] + xp[t] * cw[0:1, :]
        for k in range(1, CONV_W):
            xc = xc + xp[t + k] * cw[k:k + 1, :]
        a, u = _rglru_gates(xc, wr_ref[...], br_ref[...], wi_ref[...], bi_ref[...], lam_ref[...])
        h = a * h + u
        ya_ref[t] = h * _silu(ga_ref[t])
    for i in range(CONV_W - 1):
        cnew_ref[i] = xp[nt + i]
    hl_ref[...] = h


def _block_diag(w):
    g, bi, bj = w.shape
    eye = jnp.eye(g, dtype=w.dtype)
    return (w[:, :, None, :] * eye[:, None, :, None]).reshape(g * bi, g * bj)


def _gate_params(conv_w, conv_b, w_r, b_r, w_i, b_i, lam):
    c = conv_w.shape[-1]
    return (conv_w, conv_b.reshape(1, c), _block_diag(w_r).astype(BF16), b_r.reshape(1, c),
            _block_diag(w_i).astype(BF16), b_i.reshape(1, c), lam.reshape(1, c))


def _rglru_prompt(xa, ga, conv_buf, h0, gate_params, tb):
    b, t, c = xa.shape
    full = lambda shape: pl.BlockSpec(shape, lambda i, j: (0,) * len(shape))
    seq = pl.BlockSpec((1, tb, c), lambda i, j: (i, j, 0))
    return pl.pallas_call(
        functools.partial(_rglru_prompt_body, tb=tb),
        grid=(b, t // tb),
        in_specs=[seq, seq,
                  pl.BlockSpec((1, CONV_W - 1, c), lambda i, j: (i, 0, 0)),
                  pl.BlockSpec((1, 1, c), lambda i, j: (i, 0, 0)),
                  full((CONV_W, c)), full((1, c)), full((c, c)), full((1, c)), full((c, c)), full((1, c)),
                  full((1, c))],
        out_specs=[seq,
                   pl.BlockSpec((1, CONV_W - 1, c), lambda i, j: (i, 0, 0)),
                   pl.BlockSpec((1, 1, c), lambda i, j: (i, 0, 0))],
        out_shape=[jax.ShapeDtypeStruct((b, t, c), F32),
                   jax.ShapeDtypeStruct((b, CONV_W - 1, c), F32),
                   jax.ShapeDtypeStruct((b, 1, c), F32)],
        scratch_shapes=[pltpu.VMEM((tb + 8, c), F32), pltpu.VMEM((1, c), F32)],
        compiler_params=_cparams(("parallel", "arbitrary"), 40),
        name="rglru_prompt",
    )(xa, ga, conv_buf, h0.reshape(b, 1, c), *gate_params)


def _rglru_decode(xa, ga, conv_buf, h0, gate_params):
    b, t, c = xa.shape
    tm = lambda z: jnp.transpose(z, (1, 0, 2))
    ya, cnew, hl = pl.pallas_call(
        _rglru_decode_body,
        out_shape=[jax.ShapeDtypeStruct((t, b, c), F32),
                   jax.ShapeDtypeStruct((CONV_W - 1, b, c), F32),
                   jax.ShapeDtypeStruct((b, c), F32)],
        compiler_params=pltpu.CompilerParams(vmem_limit_bytes=40 * MIB),
        name="rglru_decode",
    )(tm(xa), tm(ga), tm(conv_buf), h0, *gate_params)
    return tm(ya), tm(cnew), hl


TABLE_VREGS = 16


def _lane_chunk(x):
    return next(c for c in (2048, 1024, 512, 256, 128) if x % c == 0)


def _table_by_column_body(vals_ref, code_ref, out_ref, *, rc):
    base = pl.program_id(0) * N_BUCKETS

    def chunk(ci, carry):
        r0 = pl.multiple_of(ci * rc, rc)
        code = code_ref[pl.ds(r0, rc), :]
        look = lambda k, out: jnp.where(code == k, vals_ref[base + k], out)
        out_ref[0, pl.ds(r0, rc), :] = lax.fori_loop(0, N_BUCKETS, look, jnp.full(code.shape, NEG, F32))
        return carry

    lax.fori_loop(0, code_ref.shape[0] // rc, chunk, 0)


def _table_by_column(rel_bias, code):
    r, x = code.shape
    n_col = rel_bias.shape[1]
    rc = max(8, TABLE_VREGS * 1024 // x)
    return pl.pallas_call(
        functools.partial(_table_by_column_body, rc=rc),
        grid=(n_col,),
        in_specs=[pl.BlockSpec(memory_space=pltpu.SMEM), pl.BlockSpec((r, x), lambda g: (0, 0))],
        out_specs=pl.BlockSpec((1, r, x), lambda g: (g, 0, 0)),
        out_shape=jax.ShapeDtypeStruct((n_col, r, x), F32),
        compiler_params=_cparams(("parallel",), 32),
        name="bias_table_by_column",
    )(rel_bias.T.reshape(-1), code)


def _table_by_row_body(rbt_ref, code_ref, out_ref, *, lc):
    rbt = rbt_ref[...]
    n_col = rbt.shape[0]

    def group(gi, carry):
        r0 = pl.multiple_of(gi * n_col, n_col)
        for x0 in range(0, code_ref.shape[1], lc):
            code = code_ref[pl.ds(r0, n_col), x0:x0 + lc]
            out = jnp.full(code.shape, NEG, F32)
            for k in range(N_BUCKETS):
                out = jnp.where(code == k, rbt[:, k:k + 1], out)
            out_ref[pl.ds(r0, n_col), x0:x0 + lc] = out
        return carry

    lax.fori_loop(0, code_ref.shape[0] // n_col, group, 0)


def _table_by_row(rel_bias, code):
    r, x = code.shape
    return pl.pallas_call(
        functools.partial(_table_by_row_body, lc=_lane_chunk(x)),
        out_shape=jax.ShapeDtypeStruct((r, x), F32),
        name="bias_table_by_row",
    )(rel_bias.T, code)


def _bucket_code(dist, valid):
    return jnp.where(jnp.asarray(valid), _t5_bucket(jnp.asarray(dist, jnp.int32)), -1)


def _rows(start, dil):
    return pl.ds(start, BAND, stride=dil) if dil > 1 else pl.ds(start, BAND)


def _dil_prompt_body(q_ref, k_ref, v_ref, gb_ref, bt_ref, y_ref, k_scr, v_scr, o_scr, lse_scr, *, unit, n_heads):
    n = pl.program_id(2)

    @pl.when(n == 0)
    def _():
        k_scr[0:unit, :] = jnp.zeros((unit, k_scr.shape[1]), F32)
        v_scr[0:unit, :] = jnp.zeros((unit, v_scr.shape[1]), F32)

    k_scr[unit:2 * unit, :] = k_ref[0]
    v_scr[unit:2 * unit, :] = v_ref[0]
    for g, (_, dil) in enumerate(DIL_PATTERNS):
        span = BAND * dil

        def block(c, carry, g=g, dil=dil, span=span):
            sub = c // dil
            base = sub * span + c % dil
            if dil == 1:
                base = pl.multiple_of(base, BAND)
            has_prev = jnp.logical_or(n > 0, sub > 0)
            q = q_ref[0, _rows(base, dil), :] * (DH_B ** -0.5)
            kc = k_scr[_rows(unit + base, dil), :]
            kp = k_scr[_rows(unit + base - span, dil), :]
            vc = v_scr[_rows(unit + base, dil), :]
            vp = v_scr[_rows(unit + base - span, dil), :]
            outs, lses = [], []
            for h in range(n_heads):
                sl = slice(h * DH_B, (h + 1) * DH_B)
                qh = q[:, sl].astype(BF16)
                sp = _dot_nt(qh, kp[:, sl].astype(BF16)) + bt_ref[h, g, :, 0:BAND]
                sp = jnp.where(has_prev, sp, NEG)
                sc = _dot_nt(qh, kc[:, sl].astype(BF16)) + bt_ref[h, g, :, BAND:2 * BAND]
                m = jnp.maximum(jnp.max(sp, axis=-1, keepdims=True), jnp.max(sc, axis=-1, keepdims=True))
                pp = jnp.exp(sp - m)
                pc = jnp.exp(sc - m)
                l = jnp.sum(pp, axis=-1, keepdims=True) + jnp.sum(pc, axis=-1, keepdims=True)
                o = (_dot(pp.astype(BF16), vp[:, sl].astype(BF16))
                     + _dot(pc.astype(BF16), vc[:, sl].astype(BF16))) / l
                outs.append(o)
                lses.append(jnp.broadcast_to(m + jnp.log(l), o.shape))
            o_scr[g, _rows(base, dil), :] = jnp.concatenate(outs, axis=-1)
            lse_scr[g, _rows(base, dil), :] = jnp.concatenate(lses, axis=-1)
            return carry

        lax.fori_loop(0, unit // BAND, block, 0)

    lse1, lse2, lse3 = lse_scr[0], lse_scr[1], lse_scr[2]
    m = jnp.maximum(jnp.maximum(lse1, lse2), lse3)
    e1, e2, e3 = jnp.exp(lse1 - m), jnp.exp(lse2 - m), jnp.exp(lse3 - m)
    ob = (e1 * o_scr[0] + e2 * o_scr[1] + e3 * o_scr[2]) / (e1 + e2 + e3)
    y_ref[0] = ob * _silu(gb_ref[0])
    k_scr[0:unit, :] = k_ref[0]
    v_scr[0:unit, :] = v_ref[0]


def _dil_prompt(q, k, v, gb, rel_bias):
    b, s, w = q.shape
    unit = BAND * DIL_PATTERNS[-1][1]
    lw = 2 * DH_B
    n_heads = w // DH_B
    idx = np.arange(BAND)[:, None] + BAND - np.arange(2 * BAND)[None, :]
    ok = (idx >= 0) & (idx <= BAND)
    code = jnp.concatenate([_bucket_code(dil * np.clip(idx, 0, BAND), ok) for _, dil in DIL_PATTERNS])
    tiles = _table_by_column(rel_bias, code).reshape(n_heads, len(DIL_PATTERNS), BAND, 2 * BAND)
    seq = pl.BlockSpec((1, unit, lw), lambda i, hp, n: (i, n, hp))
    return pl.pallas_call(
        functools.partial(_dil_prompt_body, unit=unit, n_heads=lw // DH_B),
        grid=(b, w // lw, s // unit),
        in_specs=[seq, seq, seq, seq,
                  pl.BlockSpec((lw // DH_B, len(DIL_PATTERNS), BAND, 2 * BAND), lambda i, hp, n: (hp, 0, 0, 0))],
        out_specs=seq,
        out_shape=jax.ShapeDtypeStruct((b, s, w), F32),
        scratch_shapes=[pltpu.VMEM((2 * unit, lw), F32), pltpu.VMEM((2 * unit, lw), F32),
                        pltpu.VMEM((len(DIL_PATTERNS), unit, lw), F32),
                        pltpu.VMEM((len(DIL_PATTERNS), unit, lw), F32)],
        compiler_params=_cparams(("parallel", "parallel", "arbitrary"), 40),
        name="dil_prompt",
    )(q, k, v, gb, tiles)


def _out_proj_body(x_ref, ya_ref, yb_ref, wa_ref, wb_ref, *rest, final_norm):
    y = x_ref[...] + _dot(ya_ref[...].astype(BF16), wa_ref[...]) + _dot(yb_ref[...].astype(BF16), wb_ref[...])
    if final_norm:
        y = _rms(y, rest[0][...])
    rest[-1][...] = y


def _out_proj(x, ya, yb, w_out, final_g, tm):
    t, d = x.shape
    ca, cb = ya.shape[1], yb.shape[1]
    row = lambda c: pl.BlockSpec((tm, c), lambda i: (i, 0))
    const = lambda shape: pl.BlockSpec(shape, lambda i: (0, 0))
    wa = w_out[:ca].astype(BF16)
    wb = w_out[ca:].astype(BF16)
    args = [x, ya, yb, wa, wb]
    in_specs = [row(d), row(ca), row(cb), const((ca, d)), const((cb, d))]
    if final_g is not None:
        args.append(final_g.reshape(1, d))
        in_specs.append(const((1, d)))
    return pl.pallas_call(
        functools.partial(_out_proj_body, final_norm=final_g is not None),
        grid=(t // tm,),
        in_specs=in_specs,
        out_specs=row(d),
        out_shape=jax.ShapeDtypeStruct((t, d), F32),
        compiler_params=_cparams(("parallel",), 48),
        name="out_proj",
    )(*args)


def _lam_from(lamc, lam_init):
    s01 = jnp.sum(lamc[0:1, :] * lamc[1:2, :], axis=-1, keepdims=True)
    s23 = jnp.sum(lamc[2:3, :] * lamc[3:4, :], axis=-1, keepdims=True)
    return jnp.exp(s01) - jnp.exp(s23) + lam_init


def _diff_attn_body(q_ref, k_ref, v_ref, g_ref, bt_ref, subln_ref, lamc_ref, o_ref, *, tq, n_const, lam_init):
    qi = pl.program_id(2)
    q = q_ref[0] * (DH_C ** -0.5)
    lane = lax.broadcasted_iota(jnp.int32, q.shape, 1)
    q0 = jnp.where(lane < DH_C, q, 0.0).astype(BF16)
    q1 = jnp.where(lane >= DH_C, q, 0.0).astype(BF16)
    dv = v_ref.shape[-1]

    def update(s, carry, v):
        m, l, acc = carry
        m_new = jnp.maximum(m, jnp.max(s, axis=-1, keepdims=True))
        alpha = jnp.exp(m - m_new)
        p = jnp.exp(s - m_new)
        l = alpha * l + jnp.sum(p, axis=-1, keepdims=True)
        acc = alpha * acc + _dot(p.astype(BF16), v)
        return m_new, l, acc

    def body(kj, carry):
        c0, c1 = carry
        start = pl.multiple_of(kj * tq, tq)
        k = k_ref[0, pl.ds(start, tq), :].astype(BF16)
        v = v_ref[0, pl.ds(start, tq), :].astype(BF16)
        d = jnp.minimum(qi - kj, n_const)
        c0 = update(_dot_nt(q0, k) + bt_ref[0, d], c0, v)
        c1 = update(_dot_nt(q1, k) + bt_ref[1, d], c1, v)
        return c0, c1

    init = (jnp.full((tq, 1), NEG, F32), jnp.zeros((tq, 1), F32), jnp.zeros((tq, dv), F32))
    (_, l0, a0), (_, l1, a1) = lax.fori_loop(0, qi + 1, body, (init, init))
    lam = _lam_from(lamc_ref[...], lam_init)
    o = a0 / l0 - lam * (a1 / l1)
    oc = _rms(o, subln_ref[...]) * (1.0 - lam_init)
    o_ref[0] = oc * _silu(g_ref[0])


def _diff_bias_tiles(rel_bias, tq, n_const, n_heads):
    dist = (np.arange(n_const + 1)[:, None, None] * tq + np.arange(tq)[None, :, None]
            - np.arange(tq)[None, None, :]).reshape((n_const + 1) * tq, tq)
    tiles = _table_by_column(rel_bias, _bucket_code(np.clip(dist, 0, None), dist >= 0))
    return tiles.reshape(2, n_heads, n_const + 1, tq, tq)


def _const_tile_index(tq):
    exact = N_BUCKETS // 2
    sat = math.ceil(exact * (T5_MAX_DIST / exact) ** ((N_BUCKETS - 1 - exact) / (N_BUCKETS - exact))) + 1
    return -(-(sat + tq - 1) // tq)


def _diff_attn_prompt(q, k, v, g, rel_bias, subln, lamc, lam_init, tq):
    b, s, w = q.shape
    hw = 2 * DH_C
    n_heads = w // hw
    n_const = _const_tile_index(tq)
    tiles = _diff_bias_tiles(rel_bias, tq, n_const, n_heads)
    qspec = pl.BlockSpec((1, tq, hw), lambda i, h, j: (i, j, h))
    kvspec = pl.BlockSpec((1, s, hw), lambda i, h, j: (i, 0, h))
    return pl.pallas_call(
        functools.partial(_diff_attn_body, tq=tq, n_const=n_const, lam_init=lam_init),
        grid=(b, n_heads, s // tq),
        in_specs=[qspec, kvspec, kvspec, qspec,
                  pl.BlockSpec((2, None, n_const + 1, tq, tq), lambda i, h, j: (0, h, 0, 0, 0)),
                  pl.BlockSpec((1, hw), lambda i, h, j: (0, 0)),
                  pl.BlockSpec((4, DH_C), lambda i, h, j: (0, 0))],
        out_specs=qspec,
        out_shape=jax.ShapeDtypeStruct((b, s, w), F32),
        compiler_params=_cparams(("parallel", "parallel", "arbitrary"), 48),
        name="diff_attn_prompt",
    )(q, k, v, g, tiles, subln.reshape(1, hw), lamc)


def _hgrn2_body(q_ref, f_ref, v_ref, g_ref, lb_ref, gn_ref, s0_ref, y_ref, sl_ref, st_scr, o_scr, *, tb, chunk):
    j = pl.program_id(2)

    @pl.when(j == 0)
    def _():
        st_scr[...] = s0_ref[0, 0].T

    lb = lb_ref[...]
    f = lb + (1.0 - lb) * _sigmoid(f_ref[0])
    log_f = jnp.log(f)
    kk = 1.0 - f
    qf = _silu(q_ref[0]) * (DK_D ** -0.5)
    v = v_ref[0].astype(BF16)

    row = lax.broadcasted_iota(jnp.int32, log_f.shape, 0)
    pos = row % chunk
    bc = log_f
    tot = log_f
    s = 1
    while s < chunk:
        bc = jnp.where(pos >= s, bc + pltpu.roll(bc, s, 0), bc)
        tot = tot + jnp.where((pos & s) == 0, pltpu.roll(tot, tb - s, 0), pltpu.roll(tot, s, 0))
        s *= 2
    qd = (qf * jnp.exp(bc)).astype(BF16)
    kd = (kk * jnp.exp(-bc)).astype(BF16)
    kl = (kk * jnp.exp(tot - bc)).astype(BF16)
    gdec = jnp.exp(tot)

    att = _dot_nt(qd, kd)
    r2 = lax.broadcasted_iota(jnp.int32, att.shape, 0)
    c2 = lax.broadcasted_iota(jnp.int32, att.shape, 1)
    att = jnp.where((r2 // chunk == c2 // chunk) & (c2 <= r2), att, 0.0)
    o_scr[...] = _dot(att.astype(BF16), v)

    st = st_scr[...]
    for n in range(tb // chunk):
        sl = slice(n * chunk, (n + 1) * chunk)
        o_scr[sl, :] += _dot_nt(qd[sl], st.astype(BF16))
        st = st * gdec[n * chunk:n * chunk + 1, :] + _dot_tn(v[sl], kl[sl])
    st_scr[...] = st

    @pl.when(j == pl.num_programs(2) - 1)
    def _():
        sl_ref[0, 0] = st.T

    y_ref[0] = _rms(o_scr[...], gn_ref[...]) * _silu(g_ref[0])


def _hgrn2_prompt(q, f, v, g, lb, gnorm, s0, tb):
    b, s, w = q.shape
    n_heads = w // DK_D
    seq = pl.BlockSpec((1, tb, DK_D), lambda i, h, j: (i, j, h))
    st = pl.BlockSpec((1, 1, DK_D, DV_D), lambda i, h, j: (i, h, 0, 0))
    return pl.pallas_call(
        functools.partial(_hgrn2_body, tb=tb, chunk=CHUNK_D),
        grid=(b, n_heads, s // tb),
        in_specs=[seq, seq, seq, seq,
                  pl.BlockSpec((1, DK_D), lambda i, h, j: (0, h)),
                  pl.BlockSpec((1, DV_D), lambda i, h, j: (0, 0)),
                  st],
        out_specs=[seq, st],
        out_shape=[jax.ShapeDtypeStruct((b, s, w), F32),
                   jax.ShapeDtypeStruct((b, n_heads, DK_D, DV_D), F32)],
        scratch_shapes=[pltpu.VMEM((DV_D, DK_D), F32), pltpu.VMEM((tb, DV_D), F32)],
        compiler_params=_cparams(("parallel", "parallel", "arbitrary"), 32),
        name="hgrn2_prompt",
    )(q, f, v, g, lb.reshape(1, w), gnorm.reshape(1, DV_D), s0)


def _dil_decode_body(q_ref, kn_ref, vn_ref, gb_ref, kt_ref, ks_ref, vt_ref, vs_ref, tab_ref, tabn_ref, y_ref,
                     *, n1, n2, n3):
    flat = lambda ref: ref[0].reshape(-1, DH_B)
    qt = (flat(q_ref) * (DH_B ** -0.5)).astype(BF16)
    kt, vt = flat(kt_ref).astype(BF16), flat(vt_ref).astype(BF16)
    ks, vs = flat(ks_ref).astype(BF16), flat(vs_ref).astype(BF16)
    kn, vn = flat(kn_ref).astype(BF16), flat(vn_ref).astype(BF16)
    rows = qt.shape[0]

    st = _dot_nt(qt, kt)
    sn = _dot_nt(qt, kn)
    parts = [st[:, n2 - n1:] + tab_ref[:, 0:n1],
             st + tab_ref[:, n1:n1 + n2],
             _dot_nt(qt, ks) + tab_ref[:, n1 + n2:n1 + n2 + n3]]
    parts += [sn + tabn_ref[:, g * rows:(g + 1) * rows] for g in range(len(DIL_PATTERNS))]
    m = parts[0].max(axis=-1, keepdims=True)
    for p in parts[1:]:
        m = jnp.maximum(m, p.max(axis=-1, keepdims=True))
    ps = [jnp.exp(p - m) for p in parts]
    l = ps[0].sum(axis=-1, keepdims=True)
    for p in ps[1:]:
        l = l + p.sum(axis=-1, keepdims=True)
    pn = (ps[3] + ps[4] + ps[5]).astype(BF16)
    o = (_dot(ps[0].astype(BF16), vt[n2 - n1:]) + _dot(ps[1].astype(BF16), vt) + _dot(ps[2].astype(BF16), vs)
         + _dot(pn, vn))
    y = (o / l) * _silu(flat(gb_ref))
    y_ref[0] = y.reshape(y_ref.shape[1:])


def _dil_decode(q, k, v, gb, win_k, win_v, rel_bias):
    b, nt, n_heads, dh = q.shape
    l_buf = win_k.shape[1]
    rows = nt * n_heads
    (_, d1), (_, d2), (_, d3) = DIL_PATTERNS
    assert l_buf == BAND * d3 and nt <= d2 and d1 == 1 and rel_bias.shape[1] == n_heads
    n_tail = BAND * d2
    n1, n2, n3 = BAND * n_heads, n_tail * n_heads, BAND * nt * n_heads

    tt = np.arange(rows)[:, None] // n_heads
    hq = np.arange(rows)[:, None] % n_heads
    pos = lambda n: np.arange(n)[None, :] // n_heads
    same = lambda n: np.arange(n)[None, :] % n_heads == hq
    j1 = BAND + tt - pos(n1)
    c1 = _bucket_code(np.clip(j1, 0, None), same(n1) & (j1 <= BAND))
    dd = n_tail + tt - pos(n2)
    c2 = _bucket_code(np.clip(dd, 0, None), same(n2) & (dd % d2 == 0) & (dd >= d2) & (dd <= BAND * d2))
    m3, t3 = pos(n3) // nt, pos(n3) % nt
    c3 = _bucket_code(d3 * (BAND - m3) + 0 * tt, same(n3) & (t3 == tt))
    jn = tt - pos(rows)
    cn = [_bucket_code(np.clip(jn, 0, None), same(rows) & (jn >= 0)),
          _bucket_code(0 * jn, same(rows) & (jn == 0)), _bucket_code(0 * jn, same(rows) & (jn == 0))]
    cn.append(jnp.full((rows, 128 - len(cn) * rows), -1, jnp.int32))
    tab = _table_by_row(rel_bias, jnp.concatenate([c1, c2, c3], axis=1))
    tabn = _table_by_row(rel_bias, jnp.concatenate(cn, axis=1))

    new = pl.BlockSpec((1, nt, n_heads, dh), lambda i: (i, 0, 0, 0))
    tail = pl.BlockSpec((1, n_tail, n_heads, dh), lambda i: (i, l_buf // n_tail - 1, 0, 0))
    strided = pl.BlockSpec((1, BAND, nt, n_heads, dh), lambda i: (i, 0, 0, 0, 0))
    const = lambda a: pl.BlockSpec(a.shape, lambda i: (0,) * a.ndim)
    by_residue = lambda z: z.reshape(b, BAND, d3, n_heads, dh)
    return pl.pallas_call(
        functools.partial(_dil_decode_body, n1=n1, n2=n2, n3=n3),
        grid=(b,),
        in_specs=[new, new, new, new, tail, strided, tail, strided, const(tab), const(tabn)],
        out_specs=new,
        out_shape=jax.ShapeDtypeStruct((b, nt, n_heads, dh), F32),
        compiler_params=_cparams(("parallel",), 40),
        name="dil_decode",
    )(q, k, v, gb, win_k, by_residue(win_k), win_v, by_residue(win_v), tab, tabn)


def _diff_decode_body(pt_ref, q_ref, kn_ref, vn_ref, g_ref, tab_ref, tabn_ref, subln_ref, lamc_ref, *rest,
                      nt, n_heads, n_pages, lam_init):
    del pt_ref
    k_pages, v_pages, y_ref = rest[:n_pages], rest[n_pages:2 * n_pages], rest[2 * n_pages]
    w = q_ref.shape[-1]
    hw = 2 * DH_C
    grp = 2 * n_heads
    rows = nt * grp
    lane = lax.broadcasted_iota(jnp.int32, (rows, w), 1)
    row = lax.broadcasted_iota(jnp.int32, (rows, w), 0)
    row_map, row_head = (row % grp) // n_heads, row % n_heads
    q = q_ref[0] * (DH_C ** -0.5)
    qe = jnp.broadcast_to(q[:, None, :], (nt, grp, w)).reshape(rows, w)
    qe = jnp.where(lane // DH_C == 2 * row_head + row_map, qe, 0.0).astype(BF16)

    ss = [_dot_nt(qe, k_pages[p][0].astype(BF16)) + tab_ref[:, p * PAGE:(p + 1) * PAGE] for p in range(n_pages)]
    sn = _dot_nt(qe, kn_ref[0].astype(BF16)) + tabn_ref[:, 0:kn_ref.shape[1]]
    m = sn.max(axis=-1, keepdims=True)
    for s in ss:
        m = jnp.maximum(m, s.max(axis=-1, keepdims=True))
    ps = [jnp.exp(s - m) for s in ss]
    pn = jnp.exp(sn - m)
    l = pn.sum(axis=-1, keepdims=True)
    for p in ps:
        l = l + p.sum(axis=-1, keepdims=True)
    lam = _lam_from(lamc_ref[...], lam_init)
    coef = jnp.where(row_map[:, 0:1] == 0, 1.0, -lam) / l
    o = _dot((pn * coef).astype(BF16), vn_ref[0].astype(BF16))
    for p in range(n_pages):
        o = o + _dot((ps[p] * coef).astype(BF16), v_pages[p][0].astype(BF16))
    om = jnp.where(lane // hw == row_head, o, 0.0)
    oc = jnp.sum(om.reshape(nt, grp, w), axis=1)
    subln = subln_ref[...]
    oc = jnp.concatenate([_rms(oc[:, h * hw:(h + 1) * hw], subln) for h in range(n_heads)], axis=-1)
    y_ref[0] = oc * (1.0 - lam_init) * _silu(g_ref[0])


def _diff_decode(q, k, v, g, cache_k, cache_v, page_table, rel_bias, subln, lamc, lam_init):
    b, nt, w = q.shape
    hw = 2 * DH_C
    n_heads = w // hw
    n_pages = page_table.shape[1]
    past = n_pages * PAGE
    rows = nt * 2 * n_heads
    assert rel_bias.shape[1] == 2 * n_heads
    tt = np.arange(rows)[:, None] // (2 * n_heads)
    tab = _table_by_row(rel_bias, _bucket_code(past + tt - np.arange(past)[None, :], True))
    tn = np.arange(128)[None, :]
    tabn = _table_by_row(rel_bias, _bucket_code(np.clip(tt - tn, 0, None), (tn <= tt) & (tn < nt)))
    pad8 = lambda z: jnp.pad(z, ((0, 0), (0, 8 - nt), (0, 0)))

    per_b = lambda r, c: pl.BlockSpec((1, r, c), lambda i, pt: (i, 0, 0))
    const = lambda *shape: pl.BlockSpec(shape, lambda i, pt: (0,) * len(shape))
    page = lambda p: pl.BlockSpec((1, PAGE, w), lambda i, pt, p=p: (pt[i * n_pages + p], 0, 0))
    grid_spec = pltpu.PrefetchScalarGridSpec(
        num_scalar_prefetch=1,
        grid=(b,),
        in_specs=[per_b(nt, w), per_b(8, w), per_b(8, w), per_b(nt, w),
                  const(rows, past), const(rows, 128), const(1, hw), const(4, DH_C)]
                 + [page(p) for p in range(n_pages)] * 2,
        out_specs=per_b(nt, w),
    )
    return pl.pallas_call(
        functools.partial(_diff_decode_body, nt=nt, n_heads=n_heads, n_pages=n_pages, lam_init=lam_init),
        grid_spec=grid_spec,
        out_shape=jax.ShapeDtypeStruct((b, nt, w), F32),
        compiler_params=_cparams(("arbitrary",), 40),
        name="diff_decode",
    )(page_table.reshape(-1), q, pad8(k), pad8(v), g, tab, tabn, subln.reshape(1, hw), lamc,
      *([cache_k] * n_pages), *([cache_v] * n_pages))


def _hgrn2_decode_body(qt_ref, ft_ref, v_ref, g_ref, lb_ref, gn_ref, s0_ref, y_ref, sl_ref, *, nt, n_heads):
    gn = gn_ref[...]
    for h in range(n_heads):
        lb = lb_ref[h]
        f = lb + (1.0 - lb) * _sigmoid(ft_ref[0, h])
        kk = 1.0 - f
        qf = _silu(qt_ref[0, h]) * (DK_D ** -0.5)
        st = s0_ref[0, h]
        lanes = slice(h * DV_D, (h + 1) * DV_D)
        for t in range(nt):
            st = f[:, t:t + 1] * st + kk[:, t:t + 1] * v_ref[0, t:t + 1, lanes]
            o = jnp.sum(st * qf[:, t:t + 1], axis=0, keepdims=True)
            y_ref[0, t:t + 1, lanes] = _rms(o, gn) * _silu(g_ref[0, t:t + 1, lanes])
        sl_ref[0, h] = st


def _hgrn2_decode(q, f, v, g, lb, gnorm, s0):
    b, nt, w = q.shape
    n_heads = w // DK_D
    cols = lambda z: jnp.transpose(z.reshape(b, nt, n_heads, DK_D), (0, 2, 3, 1))
    per_b = lambda *shape: pl.BlockSpec((1,) + shape, lambda i: (i,) + (0,) * len(shape))
    const = lambda *shape: pl.BlockSpec(shape, lambda i: (0,) * len(shape))
    return pl.pallas_call(
        functools.partial(_hgrn2_decode_body, nt=nt, n_heads=n_heads),
        grid=(b,),
        in_specs=[per_b(n_heads, DK_D, nt), per_b(n_heads, DK_D, nt), per_b(nt, w), per_b(nt, w),
                  const(n_heads, DK_D, 1), const(1, DV_D), per_b(n_heads, DK_D, DV_D)],
        out_specs=[per_b(nt, w), per_b(n_heads, DK_D, DV_D)],
        out_shape=[jax.ShapeDtypeStruct((b, nt, w), F32),
                   jax.ShapeDtypeStruct((b, n_heads, DK_D, DV_D), F32)],
        compiler_params=_cparams(("parallel",), 32),
        name="hgrn2_decode",
    )(cols(q), cols(f), v, g, lb.reshape(n_heads, DK_D, 1), gnorm.reshape(1, DV_D), s0)


def _layer_ab(x, group, w_in, gate_params, w_out, rel_bias, state):
    b, t, d = x.shape
    c = w_out.shape[0] // 2
    xa, ga, q, k, v, gb = [z.reshape(b, t, c) for z in _norm_proj(x.reshape(b * t, d), *group, w_in, (c,) * 6,
                                                                    min(512, b * t))]
    by_head = lambda z: z.reshape(z.shape[0], z.shape[1], c // DH_B, DH_B)
    if state is None:
        ya, conv_new, h_last = _rglru_prompt(xa, ga, jnp.zeros((b, CONV_W - 1, c), F32), jnp.zeros((b, c), F32),
                                             gate_params, tb=256)
        h_last = h_last.reshape(b, c)
        yb = _dil_prompt(q, k, v, gb, rel_bias)
        keep = min(WIN_MAX, t)
        k_rows, v_rows = by_head(k[:, t - keep:]), by_head(v[:, t - keep:])
    else:
        conv_buf, h0, win_k, win_v = state
        ya, conv_new, h_last = _rglru_decode(xa, ga, conv_buf, h0, gate_params)
        k_rows, v_rows = by_head(k), by_head(v)
        yb = _dil_decode(by_head(q), k_rows, v_rows, by_head(gb), win_k, win_v, rel_bias)
    x = _out_proj(x.reshape(b * t, d), ya.reshape(b * t, c), yb.reshape(b * t, c), w_out, None, min(512, b * t))
    return x.reshape(b, t, d), conv_new, h_last, k_rows, v_rows


def _layer_cd(x, group, w_in, lamc, subln, lb, gnorm, w_out, rel_bias, lam_init, final_g, state):
    b, t, d = x.shape
    c = w_out.shape[0] // 2
    qc, kc, vc, gc, qd, fd, idd, gd = [z.reshape(b, t, c) for z in
                                       _norm_proj(x.reshape(b * t, d), *group, w_in, (c,) * 8, min(512, b * t))]
    if state is None:
        yc = _diff_attn_prompt(qc, kc, vc, gc, rel_bias, subln, lamc, lam_init, tq=256)
        yd, s_last = _hgrn2_prompt(qd, fd, idd, gd, lb, gnorm, jnp.zeros((b, c // DK_D, DK_D, DV_D), F32), tb=256)
    else:
        s0, cache_k, cache_v, page_table = state
        n_phys = cache_k.shape[0]
        yc = _diff_decode(qc, kc, vc, gc, cache_k.reshape(n_phys, PAGE, c), cache_v.reshape(n_phys, PAGE, c),
                          page_table, rel_bias, subln, lamc, lam_init)
        yd, s_last = _hgrn2_decode(qd, fd, idd, gd, lb, gnorm, s0)
    y = _out_proj(x.reshape(b * t, d), yc.reshape(b * t, c), yd.reshape(b * t, c), w_out, final_g,
                  min(512, b * t))
    heads = c // (2 * DH_C)
    return (y.reshape(b, t, d), kc.reshape(b, t, heads, 2 * DH_C), vc.reshape(b, t, heads, 2 * DH_C), s_last)


def kernel(x_prompt, x_sample, state_conv_a, state_h_a, cache_win_k, cache_win_v, cache_k_c, cache_v_c, state_s_d,
           page_table, norm_g, norm_final, rel_bias, w_in_ab, conv_w_a, conv_b_a, w_r_a, b_r_a, w_i_a, b_i_a, lam_a,
           w_out_ab, w_in_cd, lam_c, subln_c, lb_d, gnorm_d, w_out_cd):
    depth = norm_g.shape[0]
    assert depth == 2
    lb_soft = jax.nn.softmax(lb_d.astype(F32), axis=0)
    lb_all = jnp.cumsum(lb_soft, axis=0) - lb_soft[0]

    gate_params = _gate_params(conv_w_a[0], conv_b_a[0], w_r_a[0], b_r_a[0], w_i_a[0], b_i_a[0], lam_a[0])
    xp, conv_p, h_p, wk_p, wv_p = _layer_ab(x_prompt, (norm_g[0],), w_in_ab[0], gate_params, w_out_ab[0], rel_bias,
                                            None)
    xs, conv_s, h_s, wk_s, wv_s = _layer_ab(x_sample, (norm_g[0],), w_in_ab[0], gate_params, w_out_ab[0], rel_bias,
                                            (state_conv_a[0], state_h_a[0], cache_win_k[0], cache_win_v[0]))
    lam_init = 0.8 - 0.6 * math.exp(-0.3 * 1)
    cd = (w_in_cd[0], lam_c[0], subln_c[0], lb_all[1], gnorm_d[0], w_out_cd[0], rel_bias, lam_init, norm_final)
    yp, kc_p, vc_p, s_p = _layer_cd(xp, (norm_g[1],), *cd, None)
    ys, kc_s, vc_s, s_s = _layer_cd(xs, (norm_g[1],), *cd, (state_s_d[0], cache_k_c[0], cache_v_c[0], page_table))
    e = lambda z: z[None]
    return (yp, ys, e(conv_p), e(h_p), e(wk_p), e(wv_p), e(kc_p), e(vc_p), e(s_p),
            e(conv_s), e(h_s), e(wk_s), e(wv_s), e(kc_s), e(vc_s), e(s_s))
```

```python
import functools
import math

import numpy as np
import jax
import jax.numpy as jnp
from jax import lax
from jax.experimental import pallas as pl
from jax.experimental.pallas import tpu as pltpu

F32 = jnp.float32
BF16 = jnp.bfloat16

N_BLK_A = 8
CONV_W = 4
RG_C = 8.0
DH_B = 64
DIL_PATTERNS = ((128, 1), (512, 4), (2048, 16))
BAND = 128
WIN_MAX = 2048
DH_C = 64
DK_D = 128
DV_D = 128
CHUNK_D = 16
N_BUCKETS = 32
T5_MAX_DIST = 2048
PAGE = 128
EPS = 1e-6
NEG = -1e30
MIB = 2 ** 20


def _cparams(sem, vmem_mib):
    return pltpu.CompilerParams(dimension_semantics=sem, vmem_limit_bytes=vmem_mib * MIB)


def _dot(a, b):
    return jnp.dot(a, b, preferred_element_type=F32)


def _dot_nt(a, b):
    return lax.dot_general(a, b, (((1,), (1,)), ((), ())), preferred_element_type=F32)


def _dot_tn(a, b):
    return lax.dot_general(a, b, (((0,), (0,)), ((), ())), preferred_element_type=F32)


def _sigmoid(x):
    return 1.0 / (1.0 + jnp.exp(-x))


def _silu(x):
    return x * _sigmoid(x)


def _rms(x, g):
    return x * lax.rsqrt(jnp.mean(x * x, axis=-1, keepdims=True) + EPS) * g


def _t5_bucket(dist):
    n = jnp.maximum(dist, 0)
    exact = N_BUCKETS // 2
    nf = jnp.maximum(n, 1).astype(F32)
    large = exact + (jnp.log(nf / exact) / math.log(T5_MAX_DIST / exact) * (N_BUCKETS - exact)).astype(jnp.int32)
    large = jnp.minimum(large, N_BUCKETS - 1)
    return jnp.where(n < exact, n, large)


def _norm_proj_body(x_ref, g_ref, w_ref, *o_refs):
    hn = _rms(x_ref[...], g_ref[...]).astype(BF16)
    off = 0
    for o_ref in o_refs:
        w = o_ref.shape[-1]
        o_ref[...] = _dot(hn, w_ref[:, off:off + w])
        off += w


def _norm_proj(x, g, w, widths, tm):
    t, d = x.shape
    n = w.shape[1]
    return pl.pallas_call(
        _norm_proj_body,
        grid=(t // tm,),
        in_specs=[pl.BlockSpec((tm, d), lambda i: (i, 0)),
                  pl.BlockSpec((1, d), lambda i: (0, 0)),
                  pl.BlockSpec((d, n), lambda i: (0, 0))],
        out_specs=[pl.BlockSpec((tm, wd), lambda i: (i, 0)) for wd in widths],
        out_shape=[jax.ShapeDtypeStruct((t, wd), F32) for wd in widths],
        compiler_params=_cparams(("parallel",), 56),
        name="norm_proj",
    )(x, g.reshape(1, d), w.astype(BF16))


def _rglru_gates(xc, wr, br, wi, bi, lam):
    xcb = xc.astype(BF16)
    r = _sigmoid(_dot(xcb, wr) + br)
    gi = _sigmoid(_dot(xcb, wi) + bi)
    nl = -lam
    softplus = jnp.maximum(nl, 0.0) + jnp.log(1.0 + jnp.exp(-jnp.abs(nl)))
    log_a = (-RG_C) * r * softplus
    a = jnp.exp(log_a)
    u = jnp.sqrt(1.0 - jnp.exp(2.0 * log_a)) * (gi * xc)
    return a, u


def _rglru_prompt_body(xa_ref, ga_ref, cb_ref, h0_ref, cw_ref, cbias_ref, wr_ref, br_ref, wi_ref, bi_ref,
                       lam_ref, ya_ref, cnew_ref, hl_ref, xp_scr, h_scr, *, tb):
    j = pl.program_id(1)

    @pl.when(j == 0)
    def _():
        xp_scr[5:8, :] = cb_ref[0]
        h_scr[...] = h0_ref[0]

    xp_scr[8:8 + tb, :] = xa_ref[0]
    cw = cw_ref[...]
    xc = cbias_ref[...] + xp_scr[5:5 + tb, :] * cw[0:1, :]
    for k in range(1, CONV_W):
        xc = xc + xp_scr[5 + k:5 + k + tb, :] * cw[k:k + 1, :]
    tail = xp_scr[5 + tb:8 + tb, :]
    cnew_ref[0] = tail
    xp_scr[5:8, :] = tail

    a, u = _rglru_gates(xc, wr_ref[...], br_ref[...], wi_ref[...], bi_ref[...], lam_ref[...])
    row = lax.broadcasted_iota(jnp.int32, a.shape, 0)
    s = 1
    while s < tb:
        keep = row >= s
        u = jnp.where(keep, a * pltpu.roll(u, s, 0) + u, u)
        a = jnp.where(keep, a * pltpu.roll(a, s, 0), a)
        s *= 2
    h = a * h_scr[...] + u
    h_last = h[tb - 1:tb, :]
    h_scr[...] = h_last
    hl_ref[0] = h_last
    ya_ref[0] = h * _silu(ga_ref[0])


def _rglru_decode_body(xa_ref, ga_ref, cb_ref, h0_ref, cw_ref, cbias_ref, wr_ref, br_ref, wi_ref, bi_ref,
                       lam_ref, ya_ref, cnew_ref, hl_ref):
    nt = xa_ref.shape[0]
    xp = [cb_ref[i] for i in range(CONV_W - 1)] + [xa_ref[t] for t in range(nt)]
    cw = cw_ref[...]
    h = h0_ref[...]
    for t in range(nt):
        xc = cbias_ref[...] + xp[t] * cw[0:1, :]
        for k in range(1, CONV_W):
            xc = xc + xp[t + k] * cw[k:k + 1, :]
        a, u = _rglru_gates(xc, wr_ref[...], br_ref[...], wi_ref[...], bi_ref[...], lam_ref[...])
        h = a * h + u
        ya_ref[t] = h * _silu(ga_ref[t])
    for i in range(CONV_W - 1):
        cnew_ref[i] = xp[nt + i]
    hl_ref[...] = h


def _block_diag(w):
    g, bi, bj = w.shape
    eye = jnp.eye(g, dtype=w.dtype)
    return (w[:, :, None, :] * eye[:, None, :, None]).reshape(g * bi, g * bj)


def _gate_params(conv_w, conv_b, w_r, b_r, w_i, b_i, lam):
    c = conv_w.shape[-1]
    return (conv_w, conv_b.reshape(1, c), _block_diag(w_r).astype(BF16), b_r.reshape(1, c),
            _block_diag(w_i).astype(BF16), b_i.reshape(1, c), lam.reshape(1, c))


def _rglru_prompt(xa, ga, conv_buf, h0, gate_params, tb):
    b, t, c = xa.shape
    full = lambda shape: pl.BlockSpec(shape, lambda i, j: (0,) * len(shape))
    seq = pl.BlockSpec((1, tb, c), lambda i, j: (i, j, 0))
    return pl.pallas_call(
        functools.partial(_rglru_prompt_body, tb=tb),
        grid=(b, t // tb),
        in_specs=[seq, seq,
                  pl.BlockSpec((1, CONV_W - 1, c), lambda i, j: (i, 0, 0)),
                  pl.BlockSpec((1, 1, c), lambda i, j: (i, 0, 0)),
                  full((CONV_W, c)), full((1, c)), full((c, c)), full((1, c)), full((c, c)), full((1, c)),
                  full((1, c))],
        out_specs=[seq,
                   pl.BlockSpec((1, CONV_W - 1, c), lambda i, j: (i, 0, 0)),
                   pl.BlockSpec((1, 1, c), lambda i, j: (i, 0, 0))],
        out_shape=[jax.ShapeDtypeStruct((b, t, c), F32),
                   jax.ShapeDtypeStruct((b, CONV_W - 1, c), F32),
                   jax.ShapeDtypeStruct((b, 1, c), F32)],
        scratch_shapes=[pltpu.VMEM((tb + 8, c), F32), pltpu.VMEM((1, c), F32)],
        compiler_params=_cparams(("parallel", "arbitrary"), 40),
        name="rglru_prompt",
    )(xa, ga, conv_buf, h0.reshape(b, 1, c), *gate_params)


def _rglru_decode(xa, ga, conv_buf, h0, gate_params):
    b, t, c = xa.shape
    tm = lambda z: jnp.transpose(z, (1, 0, 2))
    ya, cnew, hl = pl.pallas_call(
        _rglru_decode_body,
        out_shape=[jax.ShapeDtypeStruct((t, b, c), F32),
                   jax.ShapeDtypeStruct((CONV_W - 1, b, c), F32),
                   jax.ShapeDtypeStruct((b, c), F32)],
        compiler_params=pltpu.CompilerParams(vmem_limit_bytes=40 * MIB),
        name="rglru_decode",
    )(tm(xa), tm(ga), tm(conv_buf), h0, *gate_params)
    return tm(ya), tm(cnew), hl


TABLE_VREGS = 16


def _lane_chunk(x):
    return next(c for c in (2048, 1024, 512, 256, 128) if x % c == 0)


def _table_by_column_body(vals_ref, code_ref, out_ref, *, rc):
    base = pl.program_id(0) * N_BUCKETS

    def chunk(ci, carry):
        r0 = pl.multiple_of(ci * rc, rc)
        code = code_ref[pl.ds(r0, rc), :]
        look = lambda k, out: jnp.where(code == k, vals_ref[base + k], out)
        out_ref[0, pl.ds(r0, rc), :] = lax.fori_loop(0, N_BUCKETS, look, jnp.full(code.shape, NEG, F32))
        return carry

    lax.fori_loop(0, code_ref.shape[0] // rc, chunk, 0)


def _table_by_column(rel_bias, code):
    r, x = code.shape
    n_col = rel_bias.shape[1]
    rc = min(r, max(8, TABLE_VREGS * 1024 // x))
    assert r % rc == 0
    return pl.pallas_call(
        functools.partial(_table_by_column_body, rc=rc),
        grid=(n_col,),
        in_specs=[pl.BlockSpec(memory_space=pltpu.SMEM), pl.BlockSpec((r, x), lambda g: (0, 0))],
        out_specs=pl.BlockSpec((1, r, x), lambda g: (g, 0, 0)),
        out_shape=jax.ShapeDtypeStruct((n_col, r, x), F32),
        compiler_params=_cparams(("parallel",), 32),
        name="bias_table_by_column",
    )(rel_bias.T.reshape(-1), code)


def _table_by_row_body(rbt_ref, code_ref, out_ref, *, lc):
    rbt = rbt_ref[...]
    n_col = rbt.shape[0]

    def group(gi, carry):
        r0 = pl.multiple_of(gi * n_col, n_col)
        for x0 in range(0, code_ref.shape[1], lc):
            code = code_ref[pl.ds(r0, n_col), x0:x0 + lc]
            out = jnp.full(code.shape, NEG, F32)
            for k in range(N_BUCKETS):
                out = jnp.where(code == k, rbt[:, k:k + 1], out)
            out_ref[pl.ds(r0, n_col), x0:x0 + lc] = out
        return carry

    lax.fori_loop(0, code_ref.shape[0] // n_col, group, 0)


def _table_by_row(rel_bias, code):
    r, x = code.shape
    return pl.pallas_call(
        functools.partial(_table_by_row_body, lc=_lane_chunk(x)),
        out_shape=jax.ShapeDtypeStruct((r, x), F32),
        name="bias_table_by_row",
    )(rel_bias.T, code)


def _bucket_code(dist, valid):
    return jnp.where(jnp.asarray(valid), _t5_bucket(jnp.asarray(dist, jnp.int32)), -1)


def _rows(start, dil):
    return pl.ds(start, BAND, stride=dil) if dil > 1 else pl.ds(start, BAND)


def _dil_prompt_body(q_ref, k_ref, v_ref, gb_ref, bt_ref, y_ref, k_scr, v_scr, o_scr, lse_scr, *, unit, n_heads):
    n = pl.program_id(2)

    @pl.when(n == 0)
    def _():
        k_scr[0:unit, :] = jnp.zeros((unit, k_scr.shape[1]), F32)
        v_scr[0:unit, :] = jnp.zeros((unit, v_scr.shape[1]), F32)

    k_scr[unit:2 * unit, :] = k_ref[0]
    v_scr[unit:2 * unit, :] = v_ref[0]
    for g, (_, dil) in enumerate(DIL_PATTERNS):
        span = BAND * dil

        def block(c, carry, g=g, dil=dil, span=span):
            sub = c // dil
            base = sub * span + c % dil
            if dil == 1:
                base = pl.multiple_of(base, BAND)
            has_prev = jnp.logical_or(n > 0, sub > 0)
            q = q_ref[0, _rows(base, dil), :] * (DH_B ** -0.5)
            kc = k_scr[_rows(unit + base, dil), :]
            kp = k_scr[_rows(unit + base - span, dil), :]
            vc = v_scr[_rows(unit + base, dil), :]
            vp = v_scr[_rows(unit + base - span, dil), :]
            outs, lses = [], []
            for h in range(n_heads):
                sl = slice(h * DH_B, (h + 1) * DH_B)
                qh = q[:, sl].astype(BF16)
                sp = _dot_nt(qh, kp[:, sl].astype(BF16)) + bt_ref[h, g, :, 0:BAND]
                sp = jnp.where(has_prev, sp, NEG)
                sc = _dot_nt(qh, kc[:, sl].astype(BF16)) + bt_ref[h, g, :, BAND:2 * BAND]
                m = jnp.maximum(jnp.max(sp, axis=-1, keepdims=True), jnp.max(sc, axis=-1, keepdims=True))
                pp = jnp.exp(sp - m)
                pc = jnp.exp(sc - m)
                l = jnp.sum(pp, axis=-1, keepdims=True) + jnp.sum(pc, axis=-1, keepdims=True)
                o = (_dot(pp.astype(BF16), vp[:, sl].astype(BF16))
                     + _dot(pc.astype(BF16), vc[:, sl].astype(BF16))) / l
                outs.append(o)
                lses.append(jnp.broadcast_to(m + jnp.log(l), o.shape))
            o_scr[g, _rows(base, dil), :] = jnp.concatenate(outs, axis=-1)
            lse_scr[g, _rows(base, dil), :] = jnp.concatenate(lses, axis=-1)
            return carry

        lax.fori_loop(0, unit // BAND, block, 0)

    lse1, lse2, lse3 = lse_scr[0], lse_scr[1], lse_scr[2]
    m = jnp.maximum(jnp.maximum(lse1, lse2), lse3)
    e1, e2, e3 = jnp.exp(lse1 - m), jnp.exp(lse2 - m), jnp.exp(lse3 - m)
    ob = (e1 * o_scr[0] + e2 * o_scr[1] + e3 * o_scr[2]) / (e1 + e2 + e3)
    y_ref[0] = ob * _silu(gb_ref[0])
    k_scr[0:unit, :] = k_ref[0]
    v_scr[0:unit, :] = v_ref[0]


def _dil_prompt(q, k, v, gb, rel_bias):
    b, s, w = q.shape
    unit = BAND * DIL_PATTERNS[-1][1]
    lw = 2 * DH_B
    n_heads = w // DH_B
    idx = np.arange(BAND)[:, None] + BAND - np.arange(2 * BAND)[None, :]
    ok = (idx >= 0) & (idx <= BAND)
    code = jnp.concatenate([_bucket_code(dil * np.clip(idx, 0, BAND), ok) for _, dil in DIL_PATTERNS])
    tiles = _table_by_column(rel_bias, code).reshape(n_heads, len(DIL_PATTERNS), BAND, 2 * BAND)
    seq = pl.BlockSpec((1, unit, lw), lambda i, hp, n: (i, n, hp))
    return pl.pallas_call(
        functools.partial(_dil_prompt_body, unit=unit, n_heads=lw // DH_B),
        grid=(b, w // lw, s // unit),
        in_specs=[seq, seq, seq, seq,
                  pl.BlockSpec((lw // DH_B, len(DIL_PATTERNS), BAND, 2 * BAND), lambda i, hp, n: (hp, 0, 0, 0))],
        out_specs=seq,
        out_shape=jax.ShapeDtypeStruct((b, s, w), F32),
        scratch_shapes=[pltpu.VMEM((2 * unit, lw), F32), pltpu.VMEM((2 * unit, lw), F32),
                        pltpu.VMEM((len(DIL_PATTERNS), unit, lw), F32),
                        pltpu.VMEM((len(DIL_PATTERNS), unit, lw), F32)],
        compiler_params=_cparams(("parallel", "parallel", "arbitrary"), 40),
        name="dil_prompt",
    )(q, k, v, gb, tiles)


def _out_proj_body(x_ref, ya_ref, yb_ref, wa_ref, wb_ref, *rest, final_norm):
    y = x_ref[...] + _dot(ya_ref[...].astype(BF16), wa_ref[...]) + _dot(yb_ref[...].astype(BF16), wb_ref[...])
    if final_norm:
        y = _rms(y, rest[0][...])
    rest[-1][...] = y


def _out_proj(x, ya, yb, w_out, final_g, tm):
    t, d = x.shape
    ca, cb = ya.shape[1], yb.shape[1]
    row = lambda c: pl.BlockSpec((tm, c), lambda i: (i, 0))
    const = lambda shape: pl.BlockSpec(shape, lambda i: (0, 0))
    wa = w_out[:ca].astype(BF16)
    wb = w_out[ca:].astype(BF16)
    args = [x, ya, yb, wa, wb]
    in_specs = [row(d), row(ca), row(cb), const((ca, d)), const((cb, d))]
    if final_g is not None:
        args.append(final_g.reshape(1, d))
        in_specs.append(const((1, d)))
    return pl.pallas_call(
        functools.partial(_out_proj_body, final_norm=final_g is not None),
        grid=(t // tm,),
        in_specs=in_specs,
        out_specs=row(d),
        out_shape=jax.ShapeDtypeStruct((t, d), F32),
        compiler_params=_cparams(("parallel",), 48),
        name="out_proj",
    )(*args)


def _lam_from(lamc, lam_init):
    s01 = jnp.sum(lamc[0:1, :] * lamc[1:2, :], axis=-1, keepdims=True)
    s23 = jnp.sum(lamc[2:3, :] * lamc[3:4, :], axis=-1, keepdims=True)
    return jnp.exp(s01) - jnp.exp(s23) + lam_init


LOG2E = math.log2(math.e)


def _diff_attn_body(sat_ref, q_ref, k_ref, v_ref, g_ref, bt_ref, subln_ref, lamc_ref, o_ref, kb_scr, vb_scr,
                    *, tq, n_const, lam_init):
    head, qi = pl.program_id(1), pl.program_id(2)
    dv = v_ref.shape[-1]
    tk = 2 * tq

    @pl.when(qi == 0)
    def _():
        kb_scr[...] = k_ref[0].astype(BF16)
        vb_scr[:, 0:dv] = v_ref[0].astype(BF16)
        lane = lax.broadcasted_iota(jnp.int32, (vb_scr.shape[0], dv), 1)
        vb_scr[:, dv:2 * dv] = jnp.where(lane == 0, 1.0, 0.0).astype(BF16)

    q = q_ref[0] * (DH_C ** -0.5 * LOG2E)
    lane = lax.broadcasted_iota(jnp.int32, q.shape, 1)
    qs = (jnp.where(lane < DH_C, q, 0.0).astype(BF16), jnp.where(lane >= DH_C, q, 0.0).astype(BF16))
    n_heads = pl.num_programs(1)
    sat = [sat_ref[(c * n_heads + head) * N_BUCKETS + N_BUCKETS - 1] for c in range(2)]

    def chunk(kj):
        start = pl.multiple_of(kj * tk, tk)
        return kb_scr[pl.ds(start, tk), :], vb_scr[pl.ds(start, tk), :]

    def far(kj, carry):
        kb, vb = chunk(kj)
        out = []
        for c in range(2):
            m, acc = carry[c]
            s = _dot_nt(qs[c], kb)
            m_new = jnp.maximum(m, jnp.max(s, axis=-1, keepdims=True) + sat[c])
            p = jnp.exp2(s - (m_new - sat[c]))
            out.append((m_new, jnp.exp2(m - m_new) * acc + _dot(p.astype(BF16), vb)))
        return tuple(out)

    def near(kj, carry):
        kb, vb = chunk(kj)
        d = jnp.minimum(qi - 2 * kj, n_const)
        out = []
        for c in range(2):
            m, acc = carry[c]
            s = _dot_nt(qs[c], kb)
            sa = s[:, 0:tq] + bt_ref[c, d + 1]
            sb = s[:, tq:tk] + bt_ref[c, d]
            m_new = jnp.maximum(m, jnp.maximum(jnp.max(sa, axis=-1, keepdims=True),
                                               jnp.max(sb, axis=-1, keepdims=True)))
            pv = (_dot(jnp.exp2(sa - m_new).astype(BF16), vb[0:tq])
                  + _dot(jnp.exp2(sb - m_new).astype(BF16), vb[tq:tk]))
            out.append((m_new, jnp.exp2(m - m_new) * acc + pv))
        return tuple(out)

    init = (jnp.full((tq, 1), NEG, F32), jnp.zeros((tq, 2 * dv), F32))
    n_far = jnp.maximum(qi - n_const + 1, 0) // 2
    carry = lax.fori_loop(0, n_far, far, (init, init))
    (_, acc0), (_, acc1) = lax.fori_loop(n_far, qi // 2 + 1, near, carry)
    lam = _lam_from(lamc_ref[...], lam_init)
    o = acc0[:, 0:dv] / acc0[:, dv:dv + 1] - lam * (acc1[:, 0:dv] / acc1[:, dv:dv + 1])
    oc = _rms(o, subln_ref[...]) * (1.0 - lam_init)
    o_ref[0] = oc * _silu(g_ref[0])


def _diff_bias_tiles(rel_bias, tq, n_const, n_heads):
    dist = (np.arange(-1, n_const + 1)[:, None, None] * tq + np.arange(tq)[None, :, None]
            - np.arange(tq)[None, None, :]).reshape((n_const + 2) * tq, tq)
    tiles = _table_by_column(rel_bias, _bucket_code(np.clip(dist, 0, None), dist >= 0))
    return tiles.reshape(2, n_heads, n_const + 2, tq, tq)


def _const_tile_index(tq):
    exact = N_BUCKETS // 2
    sat = math.ceil(exact * (T5_MAX_DIST / exact) ** ((N_BUCKETS - 1 - exact) / (N_BUCKETS - exact))) + 1
    return -(-(sat + tq - 1) // tq)


def _diff_attn_prompt(q, k, v, g, rel_bias, subln, lamc, lam_init, tq):
    b, s, w = q.shape
    hw = 2 * DH_C
    n_heads = w // hw
    n_const = _const_tile_index(tq)
    assert s % (2 * tq) == 0
    bias2 = rel_bias * LOG2E
    tiles = _diff_bias_tiles(bias2, tq, n_const, n_heads)
    qspec = pl.BlockSpec((1, tq, hw), lambda i, h, j: (i, j, h))
    kvspec = pl.BlockSpec((1, s, hw), lambda i, h, j: (i, 0, h))
    return pl.pallas_call(
        functools.partial(_diff_attn_body, tq=tq, n_const=n_const, lam_init=lam_init),
        grid=(b, n_heads, s // tq),
        in_specs=[pl.BlockSpec(memory_space=pltpu.SMEM), qspec, kvspec, kvspec, qspec,
                  pl.BlockSpec((2, None, n_const + 2, tq, tq), lambda i, h, j: (0, h, 0, 0, 0)),
                  pl.BlockSpec((1, hw), lambda i, h, j: (0, 0)),
                  pl.BlockSpec((4, DH_C), lambda i, h, j: (0, 0))],
        out_specs=qspec,
        out_shape=jax.ShapeDtypeStruct((b, s, w), F32),
        scratch_shapes=[pltpu.VMEM((s, hw), BF16), pltpu.VMEM((s, 2 * hw), BF16)],
        compiler_params=_cparams(("parallel", "parallel", "arbitrary"), 56),
        name="diff_attn_prompt",
    )(bias2.T.reshape(-1), q, k, v, g, tiles, subln.reshape(1, hw), lamc)


def _hgrn2_body(q_ref, f_ref, v_ref, g_ref, lb_ref, gn_ref, s0_ref, y_ref, sl_ref, st_scr, o_scr, *, tb, chunk):
    j = pl.program_id(2)

    @pl.when(j == 0)
    def _():
        st_scr[...] = s0_ref[0, 0].T

    lb = lb_ref[...]
    f = lb + (1.0 - lb) * _sigmoid(f_ref[0])
    log_f = jnp.log(f)
    kk = 1.0 - f
    qf = _silu(q_ref[0]) * (DK_D ** -0.5)
    v = v_ref[0].astype(BF16)

    row = lax.broadcasted_iota(jnp.int32, log_f.shape, 0)
    pos = row % chunk
    bc = log_f
    tot = log_f
    s = 1
    while s < chunk:
        bc = jnp.where(pos >= s, bc + pltpu.roll(bc, s, 0), bc)
        tot = tot + jnp.where((pos & s) == 0, pltpu.roll(tot, tb - s, 0), pltpu.roll(tot, s, 0))
        s *= 2
    qd = (qf * jnp.exp(bc)).astype(BF16)
    kd = (kk * jnp.exp(-bc)).astype(BF16)
    kl = (kk * jnp.exp(tot - bc)).astype(BF16)
    gdec = jnp.exp(tot)

    att = _dot_nt(qd, kd)
    r2 = lax.broadcasted_iota(jnp.int32, att.shape, 0)
    c2 = lax.broadcasted_iota(jnp.int32, att.shape, 1)
    att = jnp.where((r2 // chunk == c2 // chunk) & (c2 <= r2), att, 0.0)
    o_scr[...] = _dot(att.astype(BF16), v)

    st = st_scr[...]
    for n in range(tb // chunk):
        sl = slice(n * chunk, (n + 1) * chunk)
        o_scr[sl, :] += _dot_nt(qd[sl], st.astype(BF16))
        st = st * gdec[n * chunk:n * chunk + 1, :] + _dot_tn(v[sl], kl[sl])
    st_scr[...] = st

    @pl.when(j == pl.num_programs(2) - 1)
    def _():
        sl_ref[0, 0] = st.T

    y_ref[0] = _rms(o_scr[...], gn_ref[...]) * _silu(g_ref[0])


def _hgrn2_prompt(q, f, v, g, lb, gnorm, s0, tb):
    b, s, w = q.shape
    n_heads = w // DK_D
    seq = pl.BlockSpec((1, tb, DK_D), lambda i, h, j: (i, j, h))
    st = pl.BlockSpec((1, 1, DK_D, DV_D), lambda i, h, j: (i, h, 0, 0))
    return pl.pallas_call(
        functools.partial(_hgrn2_body, tb=tb, chunk=CHUNK_D),
        grid=(b, n_heads, s // tb),
        in_specs=[seq, seq, seq, seq,
                  pl.BlockSpec((1, DK_D), lambda i, h, j: (0, h)),
                  pl.BlockSpec((1, DV_D), lambda i, h, j: (0, 0)),
                  st],
        out_specs=[seq, st],
        out_shape=[jax.ShapeDtypeStruct((b, s, w), F32),
                   jax.ShapeDtypeStruct((b, n_heads, DK_D, DV_D), F32)],
        scratch_shapes=[pltpu.VMEM((DV_D, DK_D), F32), pltpu.VMEM((tb, DV_D), F32)],
        compiler_params=_cparams(("parallel", "parallel", "arbitrary"), 32),
        name="hgrn2_prompt",
    )(q, f, v, g, lb.reshape(1, w), gnorm.reshape(1, DV_D), s0)


ROWS8 = 8


def _dil_decode_body(q_ref, kn_ref, vn_ref, gb_ref, kt_ref, vt_ref, tab_ref, tabn_ref, y_ref, *, n_heads):
    n_pat = len(DIL_PATTERNS)
    for h in range(n_heads):
        q = (q_ref[0, h] * (DH_B ** -0.5)).astype(BF16)
        s = _dot(q, kt_ref[0, h].astype(BF16))
        sn = _dot_nt(q, kn_ref[0, h].astype(BF16))
        parts = [s + tab_ref[h, g * ROWS8:(g + 1) * ROWS8] for g in range(n_pat)]
        parts += [sn + tabn_ref[h, g * ROWS8:(g + 1) * ROWS8, 0:ROWS8] for g in range(n_pat)]
        m = parts[0].max(axis=-1, keepdims=True)
        for p in parts[1:]:
            m = jnp.maximum(m, p.max(axis=-1, keepdims=True))
        ps = [jnp.exp(p - m) for p in parts]
        l = ps[0].sum(axis=-1, keepdims=True)
        for p in ps[1:]:
            l = l + p.sum(axis=-1, keepdims=True)
        pw = (ps[0] + ps[1] + ps[2]).astype(BF16)
        pn = (ps[3] + ps[4] + ps[5]).astype(BF16)
        o = _dot_nt(pw, vt_ref[0, h].astype(BF16)) + _dot(pn, vn_ref[0, h].astype(BF16))
        y_ref[0, h] = (o / l) * _silu(gb_ref[0, h])


def _dil_decode(q, k, v, gb, win_k, win_v, rel_bias):
    b, nt, n_heads, dh = q.shape
    l_buf = win_k.shape[1]
    assert nt <= ROWS8 and rel_bias.shape[1] == n_heads
    tt = np.tile(np.arange(ROWS8), len(DIL_PATTERNS))[:, None]
    dil = np.repeat([d for _, d in DIL_PATTERNS], ROWS8)[:, None]
    live = tt < nt

    def code(dd, lo):
        return _bucket_code(np.clip(dd, 0, None), live & (dd % dil == 0) & (dd >= lo * dil) & (dd <= BAND * dil))

    tab = _table_by_column(rel_bias, code(l_buf + tt - np.arange(l_buf)[None, :], 1))
    tn = np.arange(128)[None, :]
    tabn = _table_by_column(rel_bias, jnp.where(tn < nt, code(tt - tn, 0), -1))

    by_head = lambda z: jnp.pad(jnp.transpose(z, (0, 2, 1, 3)), ((0, 0), (0, 0), (0, ROWS8 - nt), (0, 0)))
    pos_minor = lambda z: jnp.transpose(z, (0, 2, 3, 1))
    new = pl.BlockSpec((1, n_heads, ROWS8, dh), lambda i: (i, 0, 0, 0))
    win = pl.BlockSpec((1, n_heads, dh, l_buf), lambda i: (i, 0, 0, 0))
    const = lambda a: pl.BlockSpec(a.shape, lambda i: (0,) * a.ndim)
    y = pl.pallas_call(
        functools.partial(_dil_decode_body, n_heads=n_heads),
        grid=(b,),
        in_specs=[new, new, new, new, win, win, const(tab), const(tabn)],
        out_specs=new,
        out_shape=jax.ShapeDtypeStruct((b, n_heads, ROWS8, dh), F32),
        compiler_params=_cparams(("parallel",), 40),
        name="dil_decode",
    )(by_head(q), by_head(k), by_head(v), by_head(gb), pos_minor(win_k), pos_minor(win_v), tab, tabn)
    return jnp.transpose(y[:, :, :nt], (0, 2, 1, 3))


def _diff_decode_body(pt_ref, qm_ref, kn_ref, vn_ref, g_ref, tab_ref, tabn_ref, subln_ref, lamc_ref, *rest,
                      nt, n_heads, n_pages, lam_init):
    del pt_ref
    k_pages, v_pages, y_ref = rest[:n_pages], rest[n_pages:2 * n_pages], rest[2 * n_pages]
    qm = (qm_ref[0] * (DH_C ** -0.5)).astype(BF16)
    rows, hw = qm.shape
    grp = 2 * n_heads
    pw = PAGE * n_heads
    ss = [_dot_nt(qm, k_pages[p][0].astype(BF16)) + tab_ref[:, p * pw:(p + 1) * pw] for p in range(n_pages)]
    sn = _dot_nt(qm, kn_ref[0].astype(BF16)) + tabn_ref[:, 0:kn_ref.shape[1]]
    m = sn.max(axis=-1, keepdims=True)
    for s in ss:
        m = jnp.maximum(m, s.max(axis=-1, keepdims=True))
    ps = [jnp.exp(s - m) for s in ss]
    pn = jnp.exp(sn - m)
    l = pn.sum(axis=-1, keepdims=True)
    for p in ps:
        l = l + p.sum(axis=-1, keepdims=True)
    lam = _lam_from(lamc_ref[...], lam_init)
    row_map = (lax.broadcasted_iota(jnp.int32, (rows, 1), 0) % grp) // n_heads
    coef = jnp.where(row_map == 0, 1.0, -lam) / l
    o = _dot((pn * coef).astype(BF16), vn_ref[0].astype(BF16))
    for p in range(n_pages):
        o = o + _dot((ps[p] * coef).astype(BF16), v_pages[p][0].astype(BF16))
    o = o.reshape(nt, grp, hw)
    oc = o[:, 0:n_heads, :] + o[:, n_heads:grp, :]
    y_ref[0] = _rms(oc, subln_ref[...]) * (1.0 - lam_init) * _silu(g_ref[0])


def _diff_decode(q, k, v, g, cache_k, cache_v, page_table, rel_bias, subln, lamc, lam_init):
    b, nt, w = q.shape
    hw = 2 * DH_C
    n_heads = w // hw
    n_phys = cache_k.shape[0]
    n_pages = page_table.shape[1]
    past = n_pages * PAGE
    grp = 2 * n_heads
    rows = nt * grp
    assert rel_bias.shape[1] == grp
    tt = np.arange(rows)[:, None] // grp
    hq = np.arange(rows)[:, None] % n_heads
    pos = lambda n: np.arange(n)[None, :] // n_heads
    same = lambda n: np.arange(n)[None, :] % n_heads == hq
    tab = _table_by_row(rel_bias, _bucket_code(past + tt - pos(past * n_heads), same(past * n_heads)))
    jn = tt - pos(128)
    tabn = _table_by_row(rel_bias, _bucket_code(np.clip(jn, 0, None), same(128) & (jn >= 0) & (pos(128) < nt)))

    map_mask = (np.arange(hw)[None, :] // DH_C == np.arange(2)[:, None]).astype(np.float32)
    qm = q.reshape(b, nt, 1, n_heads, hw) * map_mask[None, None, :, None, :]
    by_head = lambda z: z.reshape(b, nt * n_heads, hw)

    per_b = lambda *shape: pl.BlockSpec((1,) + shape, lambda i, pt: (i,) + (0,) * len(shape))
    const = lambda *shape: pl.BlockSpec(shape, lambda i, pt: (0,) * len(shape))
    page = lambda p: pl.BlockSpec((1, PAGE * n_heads, hw), lambda i, pt, p=p: (pt[i * n_pages + p], 0, 0))
    grid_spec = pltpu.PrefetchScalarGridSpec(
        num_scalar_prefetch=1,
        grid=(b,),
        in_specs=[per_b(rows, hw), per_b(nt * n_heads, hw), per_b(nt * n_heads, hw), per_b(nt, n_heads, hw),
                  const(rows, past * n_heads), const(rows, 128), const(1, hw), const(4, DH_C)]
                 + [page(p) for p in range(n_pages)] * 2,
        out_specs=per_b(nt, n_heads, hw),
    )
    rows_of = lambda c: c.reshape(n_phys, PAGE * n_heads, hw)
    return pl.pallas_call(
        functools.partial(_diff_decode_body, nt=nt, n_heads=n_heads, n_pages=n_pages, lam_init=lam_init),
        grid_spec=grid_spec,
        out_shape=jax.ShapeDtypeStruct((b, nt, n_heads, hw), F32),
        compiler_params=_cparams(("arbitrary",), 40),
        name="diff_decode",
    )(page_table.reshape(-1), qm.reshape(b, rows, hw), by_head(k), by_head(v), g.reshape(b, nt, n_heads, hw),
      tab, tabn, subln.reshape(1, hw), lamc, *([rows_of(cache_k)] * n_pages), *([rows_of(cache_v)] * n_pages))


def _hgrn2_decode_body(qt_ref, ft_ref, v_ref, g_ref, lb_ref, gn_ref, s0_ref, y_ref, sl_ref, *, nt, n_heads):
    gn = gn_ref[...]
    for h in range(n_heads):
        lb = lb_ref[h]
        f = lb + (1.0 - lb) * _sigmoid(ft_ref[0, h])
        kk = 1.0 - f
        qf = _silu(qt_ref[0, h]) * (DK_D ** -0.5)
        st = s0_ref[0, h]
        lanes = slice(h * DV_D, (h + 1) * DV_D)
        for t in range(nt):
            st = f[:, t:t + 1] * st + kk[:, t:t + 1] * v_ref[0, t:t + 1, lanes]
            o = jnp.sum(st * qf[:, t:t + 1], axis=0, keepdims=True)
            y_ref[0, t:t + 1, lanes] = _rms(o, gn) * _silu(g_ref[0, t:t + 1, lanes])
        sl_ref[0, h] = st


def _hgrn2_decode(q, f, v, g, lb, gnorm, s0):
    b, nt, w = q.shape
    n_heads = w // DK_D
    cols = lambda z: jnp.transpose(z.reshape(b, nt, n_heads, DK_D), (0, 2, 3, 1))
    per_b = lambda *shape: pl.BlockSpec((1,) + shape, lambda i: (i,) + (0,) * len(shape))
    const = lambda *shape: pl.BlockSpec(shape, lambda i: (0,) * len(shape))
    return pl.pallas_call(
        functools.partial(_hgrn2_decode_body, nt=nt, n_heads=n_heads),
        grid=(b,),
        in_specs=[per_b(n_heads, DK_D, nt), per_b(n_heads, DK_D, nt), per_b(nt, w), per_b(nt, w),
                  const(n_heads, DK_D, 1), const(1, DV_D), per_b(n_heads, DK_D, DV_D)],
        out_specs=[per_b(nt, w), per_b(n_heads, DK_D, DV_D)],
        out_shape=[jax.ShapeDtypeStruct((b, nt, w), F32),
                   jax.ShapeDtypeStruct((b, n_heads, DK_D, DV_D), F32)],
        compiler_params=_cparams(("parallel",), 32),
        name="hgrn2_decode",
    )(cols(q), cols(f), v, g, lb.reshape(n_heads, DK_D, 1), gnorm.reshape(1, DV_D), s0)


def _layer_ab(x, group, w_in, gate_params, w_out, rel_bias, state):
    b, t, d = x.shape
    c = w_out.shape[0] // 2
    xa, ga, q, k, v, gb = [z.reshape(b, t, c) for z in _norm_proj(x.reshape(b * t, d), *group, w_in, (c,) * 6,
                                                                    min(512, b * t))]
    by_head = lambda z: z.reshape(z.shape[0], z.shape[1], c // DH_B, DH_B)
    if state is None:
        ya, conv_new, h_last = _rglru_prompt(xa, ga, jnp.zeros((b, CONV_W - 1, c), F32), jnp.zeros((b, c), F32),
                                             gate_params, tb=256)
        h_last = h_last.reshape(b, c)
        yb = _dil_prompt(q, k, v, gb, rel_bias)
        keep = min(WIN_MAX, t)
        k_rows, v_rows = by_head(k[:, t - keep:]), by_head(v[:, t - keep:])
    else:
        conv_buf, h0, win_k, win_v = state
        ya, conv_new, h_last = _rglru_decode(xa, ga, conv_buf, h0, gate_params)
        k_rows, v_rows = by_head(k), by_head(v)
        yb = _dil_decode(by_head(q), k_rows, v_rows, by_head(gb), win_k, win_v, rel_bias)
    x = _out_proj(x.reshape(b * t, d), ya.reshape(b * t, c), yb.reshape(b * t, c), w_out, None, min(512, b * t))
    return x.reshape(b, t, d), conv_new, h_last, k_rows, v_rows


def _layer_cd(x, group, w_in, lamc, subln, lb, gnorm, w_out, rel_bias, lam_init, final_g, state):
    b, t, d = x.shape
    c = w_out.shape[0] // 2
    qc, kc, vc, gc, qd, fd, idd, gd = [z.reshape(b, t, c) for z in
                                       _norm_proj(x.reshape(b * t, d), *group, w_in, (c,) * 8, min(512, b * t))]
    if state is None:
        yc = _diff_attn_prompt(qc, kc, vc, gc, rel_bias, subln, lamc, lam_init, tq=256)
        yd, s_last = _hgrn2_prompt(qd, fd, idd, gd, lb, gnorm, jnp.zeros((b, c // DK_D, DK_D, DV_D), F32), tb=256)
    else:
        s0, cache_k, cache_v, page_table = state
        yc = _diff_decode(qc, kc, vc, gc, cache_k, cache_v, page_table, rel_bias, subln, lamc, lam_init)
        yd, s_last = _hgrn2_decode(qd, fd, idd, gd, lb, gnorm, s0)
    y = _out_proj(x.reshape(b * t, d), yc.reshape(b * t, c), yd.reshape(b * t, c), w_out, final_g,
                  min(512, b * t))
    heads = c // (2 * DH_C)
    return (y.reshape(b, t, d), kc.reshape(b, t, heads, 2 * DH_C), vc.reshape(b, t, heads, 2 * DH_C), s_last)


def kernel(x_prompt, x_sample, state_conv_a, state_h_a, cache_win_k, cache_win_v, cache_k_c, cache_v_c, state_s_d,
           page_table, norm_g, norm_final, rel_bias, w_in_ab, conv_w_a, conv_b_a, w_r_a, b_r_a, w_i_a, b_i_a, lam_a,
           w_out_ab, w_in_cd, lam_c, subln_c, lb_d, gnorm_d, w_out_cd):
    depth = norm_g.shape[0]
    assert depth == 2
    lb_soft = jax.nn.softmax(lb_d.astype(F32), axis=0)
    lb_all = jnp.cumsum(lb_soft, axis=0) - lb_soft[0]

    gate_params = _gate_params(conv_w_a[0], conv_b_a[0], w_r_a[0], b_r_a[0], w_i_a[0], b_i_a[0], lam_a[0])
    xp, conv_p, h_p, wk_p, wv_p = _layer_ab(x_prompt, (norm_g[0],), w_in_ab[0], gate_params, w_out_ab[0], rel_bias,
                                            None)
    xs, conv_s, h_s, wk_s, wv_s = _layer_ab(x_sample, (norm_g[0],), w_in_ab[0], gate_params, w_out_ab[0], rel_bias,
                                            (state_conv_a[0], state_h_a[0], cache_win_k[0], cache_win_v[0]))
    lam_init = 0.8 - 0.6 * math.exp(-0.3 * 1)
    cd = (w_in_cd[0], lam_c[0], subln_c[0], lb_all[1], gnorm_d[0], w_out_cd[0], rel_bias, lam_init, norm_final)
    yp, kc_p, vc_p, s_p = _layer_cd(xp, (norm_g[1],), *cd, None)
    ys, kc_s, vc_s, s_s = _layer_cd(xs, (norm_g[1],), *cd, (state_s_d[0], cache_k_c[0], cache_v_c[0], page_table))
    e = lambda z: z[None]
    return (yp, ys, e(conv_p), e(h_p), e(wk_p), e(wv_p), e(kc_p), e(vc_p), e(s_p),
            e(conv_s), e(h_s), e(wk_s), e(wv_s), e(kc_s), e(vc_s), e(s_s))
```

```python
import functools
import math

import numpy as np
import jax
import jax.numpy as jnp
from jax import lax
from jax.experimental import pallas as pl
from jax.experimental.pallas import tpu as pltpu

F32 = jnp.float32
BF16 = jnp.bfloat16

N_BLK_A = 8
CONV_W = 4
RG_C = 8.0
DH_B = 64
DIL_PATTERNS = ((128, 1), (512, 4), (2048, 16))
BAND = 128
WIN_MAX = 2048
DH_C = 64
DK_D = 128
DV_D = 128
CHUNK_D = 16
N_BUCKETS = 32
T5_MAX_DIST = 2048
PAGE = 128
EPS = 1e-6
NEG = -1e30
MIB = 2 ** 20


def _cparams(sem, vmem_mib):
    return pltpu.CompilerParams(dimension_semantics=sem, vmem_limit_bytes=vmem_mib * MIB)


def _dot(a, b):
    return jnp.dot(a, b, preferred_element_type=F32)


def _dot_nt(a, b):
    return lax.dot_general(a, b, (((1,), (1,)), ((), ())), preferred_element_type=F32)


def _dot_tn(a, b):
    return lax.dot_general(a, b, (((0,), (0,)), ((), ())), preferred_element_type=F32)


def _sigmoid(x):
    return 1.0 / (1.0 + jnp.exp(-x))


def _silu(x):
    return x * _sigmoid(x)


def _rms(x, g):
    return x * lax.rsqrt(jnp.mean(x * x, axis=-1, keepdims=True) + EPS) * g


def _t5_bucket(dist):
    n = np.maximum(np.asarray(dist), 0)
    exact = N_BUCKETS // 2
    nf = np.maximum(n, 1).astype(np.float32)
    val = np.log(nf / np.float32(exact)) / np.float32(math.log(T5_MAX_DIST / exact)) * np.float32(N_BUCKETS - exact)
    large = np.minimum(exact + val.astype(np.int32), N_BUCKETS - 1)
    return np.where(n < exact, n, large).astype(np.int32)


def _norm_proj_body(x_ref, g_ref, w_ref, *o_refs):
    hn = _rms(x_ref[...], g_ref[...]).astype(BF16)
    off = 0
    for o_ref in o_refs:
        w = o_ref.shape[-1]
        o_ref[...] = _dot(hn, w_ref[:, off:off + w])
        off += w


def _norm_proj(x, g, w, widths, tm):
    t, d = x.shape
    n = w.shape[1]
    return pl.pallas_call(
        _norm_proj_body,
        grid=(t // tm,),
        in_specs=[pl.BlockSpec((tm, d), lambda i: (i, 0)),
                  pl.BlockSpec((1, d), lambda i: (0, 0)),
                  pl.BlockSpec((d, n), lambda i: (0, 0))],
        out_specs=[pl.BlockSpec((tm, wd), lambda i: (i, 0)) for wd in widths],
        out_shape=[jax.ShapeDtypeStruct((t, wd), F32) for wd in widths],
        compiler_params=_cparams(("parallel",), 56),
        name="norm_proj",
    )(x, g.reshape(1, d), w.astype(BF16))


def _rglru_gates(xc, wr, br, wi, bi, lam):
    xcb = xc.astype(BF16)
    r = _sigmoid(_dot(xcb, wr) + br)
    gi = _sigmoid(_dot(xcb, wi) + bi)
    nl = -lam
    softplus = jnp.maximum(nl, 0.0) + jnp.log(1.0 + jnp.exp(-jnp.abs(nl)))
    log_a = (-RG_C) * r * softplus
    a = jnp.exp(log_a)
    u = jnp.sqrt(1.0 - jnp.exp(2.0 * log_a)) * (gi * xc)
    return a, u


def _rglru_prompt_body(xa_ref, ga_ref, cb_ref, h0_ref, cw_ref, cbias_ref, wr_ref, br_ref, wi_ref, bi_ref,
                       lam_ref, ya_ref, cnew_ref, hl_ref, xp_scr, h_scr, *, tb):
    j = pl.program_id(1)

    @pl.when(j == 0)
    def _():
        xp_scr[5:8, :] = cb_ref[0]
        h_scr[...] = h0_ref[0]

    xp_scr[8:8 + tb, :] = xa_ref[0]
    cw = cw_ref[...]
    xc = cbias_ref[...] + xp_scr[5:5 + tb, :] * cw[0:1, :]
    for k in range(1, CONV_W):
        xc = xc + xp_scr[5 + k:5 + k + tb, :] * cw[k:k + 1, :]
    tail = xp_scr[5 + tb:8 + tb, :]
    cnew_ref[0] = tail
    xp_scr[5:8, :] = tail

    a, u = _rglru_gates(xc, wr_ref[...], br_ref[...], wi_ref[...], bi_ref[...], lam_ref[...])
    row = lax.broadcasted_iota(jnp.int32, a.shape, 0)
    s = 1
    while s < tb:
        keep = row >= s
        u = jnp.where(keep, a * pltpu.roll(u, s, 0) + u, u)
        a = jnp.where(keep, a * pltpu.roll(a, s, 0), a)
        s *= 2
    h = a * h_scr[...] + u
    h_last = h[tb - 1:tb, :]
    h_scr[...] = h_last
    hl_ref[0] = h_last
    ya_ref[0] = h * _silu(ga_ref[0])


def _rglru_decode_body(xa_ref, ga_ref, cb_ref, h0_ref, cw_ref, cbias_ref, wr_ref, br_ref, wi_ref, bi_ref,
                       lam_ref, ya_ref, cnew_ref, hl_ref):
    nt = xa_ref.shape[0]
    xp = [cb_ref[i] for i in range(CONV_W - 1)] + [xa_ref[t] for t in range(nt)]
    cw = cw_ref[...]
    h = h0_ref[...]
    for t in range(nt):
        xc = cbias_ref[...] + xp[t] * cw[0:1, :]
        for k in range(1, CONV_W):
            xc = xc + xp[t + k] * cw[k:k + 1, :]
        a, u = _rglru_gates(xc, wr_ref[...], br_ref[...], wi_ref[...], bi_ref[...], lam_ref[...])
        h = a * h + u
        ya_ref[t] = h * _silu(ga_ref[t])
    for i in range(CONV_W - 1):
        cnew_ref[i] = xp[nt + i]
    hl_ref[...] = h


def _block_diag(w):
    g, bi, bj = w.shape
    eye = jnp.eye(g, dtype=w.dtype)
    return (w[:, :, None, :] * eye[:, None, :, None]).reshape(g * bi, g * bj)


def _gate_params(conv_w, conv_b, w_r, b_r, w_i, b_i, lam):
    c = conv_w.shape[-1]
    return (conv_w, conv_b.reshape(1, c), _block_diag(w_r).astype(BF16), b_r.reshape(1, c),
            _block_diag(w_i).astype(BF16), b_i.reshape(1, c), lam.reshape(1, c))


def _rglru_prompt(xa, ga, conv_buf, h0, gate_params, tb):
    b, t, c = xa.shape
    full = lambda shape: pl.BlockSpec(shape, lambda i, j: (0,) * len(shape))
    seq = pl.BlockSpec((1, tb, c), lambda i, j: (i, j, 0))
    return pl.pallas_call(
        functools.partial(_rglru_prompt_body, tb=tb),
        grid=(b, t // tb),
        in_specs=[seq, seq,
                  pl.BlockSpec((1, CONV_W - 1, c), lambda i, j: (i, 0, 0)),
                  pl.BlockSpec((1, 1, c), lambda i, j: (i, 0, 0)),
                  full((CONV_W, c)), full((1, c)), full((c, c)), full((1, c)), full((c, c)), full((1, c)),
                  full((1, c))],
        out_specs=[seq,
                   pl.BlockSpec((1, CONV_W - 1, c), lambda i, j: (i, 0, 0)),
                   pl.BlockSpec((1, 1, c), lambda i, j: (i, 0, 0))],
        out_shape=[jax.ShapeDtypeStruct((b, t, c), F32),
                   jax.ShapeDtypeStruct((b, CONV_W - 1, c), F32),
                   jax.ShapeDtypeStruct((b, 1, c), F32)],
        scratch_shapes=[pltpu.VMEM((tb + 8, c), F32), pltpu.VMEM((1, c), F32)],
        compiler_params=_cparams(("parallel", "arbitrary"), 40),
        name="rglru_prompt",
    )(xa, ga, conv_buf, h0.reshape(b, 1, c), *gate_params)


def _rglru_decode(xa, ga, conv_buf, h0, gate_params):
    b, t, c = xa.shape
    tm = lambda z: jnp.transpose(z, (1, 0, 2))
    ya, cnew, hl = pl.pallas_call(
        _rglru_decode_body,
        out_shape=[jax.ShapeDtypeStruct((t, b, c), F32),
                   jax.ShapeDtypeStruct((CONV_W - 1, b, c), F32),
                   jax.ShapeDtypeStruct((b, c), F32)],
        compiler_params=pltpu.CompilerParams(vmem_limit_bytes=40 * MIB),
        name="rglru_decode",
    )(tm(xa), tm(ga), tm(conv_buf), h0, *gate_params)
    return tm(ya), tm(cnew), hl


TABLE_VREGS = 16


def _lane_chunk(x):
    return next(c for c in (2048, 1024, 512, 256, 128) if x % c == 0)


def _table_by_column_body(vals_ref, code_ref, out_ref, *, rc):
    base = pl.program_id(0) * N_BUCKETS

    def chunk(ci, carry):
        r0 = pl.multiple_of(ci * rc, rc)
        code = code_ref[pl.ds(r0, rc), :]
        look = lambda k, out: jnp.where(code == k, vals_ref[base + k], out)
        out_ref[0, pl.ds(r0, rc), :] = lax.fori_loop(0, N_BUCKETS, look, jnp.full(code.shape, NEG, F32))
        return carry

    lax.fori_loop(0, code_ref.shape[0] // rc, chunk, 0)


def _table_by_column(rel_bias, code):
    r, x = code.shape
    n_col = rel_bias.shape[1]
    rc = min(r, max(8, TABLE_VREGS * 1024 // x))
    assert r % rc == 0
    return pl.pallas_call(
        functools.partial(_table_by_column_body, rc=rc),
        grid=(n_col,),
        in_specs=[pl.BlockSpec(memory_space=pltpu.SMEM), pl.BlockSpec((r, x), lambda g: (0, 0))],
        out_specs=pl.BlockSpec((1, r, x), lambda g: (g, 0, 0)),
        out_shape=jax.ShapeDtypeStruct((n_col, r, x), F32),
        compiler_params=_cparams(("parallel",), 32),
        name="bias_table_by_column",
    )(rel_bias.T.reshape(-1), code)


def _table_by_row_body(rbt_ref, code_ref, out_ref, *, lc):
    rbt = rbt_ref[...]
    n_col = rbt.shape[0]

    def group(gi, carry):
        r0 = pl.multiple_of(gi * n_col, n_col)
        for x0 in range(0, code_ref.shape[1], lc):
            code = code_ref[pl.ds(r0, n_col), x0:x0 + lc]
            out = jnp.full(code.shape, NEG, F32)
            for k in range(N_BUCKETS):
                out = jnp.where(code == k, rbt[:, k:k + 1], out)
            out_ref[pl.ds(r0, n_col), x0:x0 + lc] = out
        return carry

    lax.fori_loop(0, code_ref.shape[0] // n_col, group, 0)


def _table_by_row(rel_bias, code):
    r, x = code.shape
    return pl.pallas_call(
        functools.partial(_table_by_row_body, lc=_lane_chunk(x)),
        out_shape=jax.ShapeDtypeStruct((r, x), F32),
        name="bias_table_by_row",
    )(rel_bias.T, code)


def _bucket_code(dist, valid):
    return np.where(valid, _t5_bucket(dist), -1).astype(np.int32)


DIL_GROUP = 4


def _rows(start, dil):
    return pl.ds(start, BAND, stride=dil) if dil > 1 else pl.ds(start, BAND)


def _dil_prompt_body(q_ref, k_ref, v_ref, gb_ref, bt_ref, y_ref, k_scr, v_scr, o_scr, lse_scr, *, unit, n_heads):
    n = pl.program_id(2)

    @pl.when(n == 0)
    def _():
        k_scr[0:unit, :] = jnp.zeros((unit, k_scr.shape[1]), F32)
        v_scr[0:unit, :] = jnp.zeros((unit, v_scr.shape[1]), F32)

    k_scr[unit:2 * unit, :] = k_ref[0]
    v_scr[unit:2 * unit, :] = v_ref[0]
    for g, (_, dil) in enumerate(DIL_PATTERNS):
        span = BAND * dil

        def blocks(ci, carry, g=g, dil=dil, span=span):
            lane_head = lax.broadcasted_iota(jnp.int32, (BAND, k_scr.shape[1]), 1) // DH_B
            ones = jnp.ones((BAND, k_scr.shape[1]), BF16)
            loaded = []
            for u in range(DIL_GROUP):
                c = ci * DIL_GROUP + u
                sub = c // dil
                base = sub * span + c % dil
                if dil == 1:
                    base = pl.multiple_of(base, BAND)
                loaded.append((base, jnp.logical_or(n > 0, sub > 0),
                               q_ref[0, _rows(base, dil), :] * (DH_B ** -0.5),
                               k_scr[_rows(unit + base, dil), :].astype(BF16),
                               k_scr[_rows(unit + base - span, dil), :].astype(BF16),
                               v_scr[_rows(unit + base, dil), :].astype(BF16),
                               v_scr[_rows(unit + base - span, dil), :].astype(BF16)))
            scores = []
            for _, has_prev, q, kc, kp, _, _ in loaded:
                for h in range(n_heads):
                    qh = jnp.where(lane_head == h, q, 0.0).astype(BF16)
                    scores.append((jnp.where(has_prev, _dot_nt(qh, kp) + bt_ref[h, g, :, 0:BAND], NEG),
                                   _dot_nt(qh, kc) + bt_ref[h, g, :, BAND:2 * BAND]))
            probs = []
            for sp, sc in scores:
                m = jnp.max(jnp.maximum(sp, sc), axis=-1, keepdims=True)
                probs.append((m, jnp.exp(sp - m).astype(BF16), jnp.exp(sc - m).astype(BF16)))
            for u, (base, _, _, _, _, vc, vp) in enumerate(loaded):
                o_sel = lse_sel = None
                for h in range(n_heads):
                    m, pp, pc = probs[u * n_heads + h]
                    l = _dot(pp, ones) + _dot(pc, ones)
                    o = (_dot(pp, vp) + _dot(pc, vc)) / l
                    lse = m + jnp.log(l)
                    o_sel = o if h == 0 else jnp.where(lane_head == h, o, o_sel)
                    lse_sel = lse if h == 0 else jnp.where(lane_head == h, lse, lse_sel)
                o_scr[g, _rows(base, dil), :] = o_sel
                lse_scr[g, _rows(base, dil), :] = lse_sel
            return carry

        lax.fori_loop(0, unit // BAND // DIL_GROUP, blocks, 0)

    lse1, lse2, lse3 = lse_scr[0], lse_scr[1], lse_scr[2]
    m = jnp.maximum(jnp.maximum(lse1, lse2), lse3)
    e1, e2, e3 = jnp.exp(lse1 - m), jnp.exp(lse2 - m), jnp.exp(lse3 - m)
    ob = (e1 * o_scr[0] + e2 * o_scr[1] + e3 * o_scr[2]) / (e1 + e2 + e3)
    y_ref[0] = ob * _silu(gb_ref[0])
    k_scr[0:unit, :] = k_ref[0]
    v_scr[0:unit, :] = v_ref[0]


def _dil_prompt(q, k, v, gb, rel_bias):
    b, s, w = q.shape
    unit = BAND * DIL_PATTERNS[-1][1]
    lw = 2 * DH_B
    n_heads = w // DH_B
    idx = np.arange(BAND)[:, None] + BAND - np.arange(2 * BAND)[None, :]
    ok = (idx >= 0) & (idx <= BAND)
    code = np.concatenate([_bucket_code(dil * np.clip(idx, 0, BAND), ok) for _, dil in DIL_PATTERNS])
    tiles = _table_by_column(rel_bias, code).reshape(n_heads, len(DIL_PATTERNS), BAND, 2 * BAND)
    seq = pl.BlockSpec((1, unit, lw), lambda i, hp, n: (i, n, hp))
    return pl.pallas_call(
        functools.partial(_dil_prompt_body, unit=unit, n_heads=lw // DH_B),
        grid=(b, w // lw, s // unit),
        in_specs=[seq, seq, seq, seq,
                  pl.BlockSpec((lw // DH_B, len(DIL_PATTERNS), BAND, 2 * BAND), lambda i, hp, n: (hp, 0, 0, 0))],
        out_specs=seq,
        out_shape=jax.ShapeDtypeStruct((b, s, w), F32),
        scratch_shapes=[pltpu.VMEM((2 * unit, lw), F32), pltpu.VMEM((2 * unit, lw), F32),
                        pltpu.VMEM((len(DIL_PATTERNS), unit, lw), F32),
                        pltpu.VMEM((len(DIL_PATTERNS), unit, lw), F32)],
        compiler_params=_cparams(("parallel", "parallel", "arbitrary"), 40),
        name="dil_prompt",
    )(q, k, v, gb, tiles)


def _out_proj_body(x_ref, ya_ref, yb_ref, wa_ref, wb_ref, *rest, final_norm):
    y = x_ref[...] + _dot(ya_ref[...].astype(BF16), wa_ref[...]) + _dot(yb_ref[...].astype(BF16), wb_ref[...])
    if final_norm:
        y = _rms(y, rest[0][...])
    rest[-1][...] = y


def _out_proj(x, ya, yb, w_out, final_g, tm):
    t, d = x.shape
    ca, cb = ya.shape[1], yb.shape[1]
    row = lambda c: pl.BlockSpec((tm, c), lambda i: (i, 0))
    const = lambda shape: pl.BlockSpec(shape, lambda i: (0, 0))
    wa = w_out[:ca].astype(BF16)
    wb = w_out[ca:].astype(BF16)
    args = [x, ya, yb, wa, wb]
    in_specs = [row(d), row(ca), row(cb), const((ca, d)), const((cb, d))]
    if final_g is not None:
        args.append(final_g.reshape(1, d))
        in_specs.append(const((1, d)))
    return pl.pallas_call(
        functools.partial(_out_proj_body, final_norm=final_g is not None),
        grid=(t // tm,),
        in_specs=in_specs,
        out_specs=row(d),
        out_shape=jax.ShapeDtypeStruct((t, d), F32),
        compiler_params=_cparams(("parallel",), 48),
        name="out_proj",
    )(*args)


def _lam_from(lamc, lam_init):
    s01 = jnp.sum(lamc[0:1, :] * lamc[1:2, :], axis=-1, keepdims=True)
    s23 = jnp.sum(lamc[2:3, :] * lamc[3:4, :], axis=-1, keepdims=True)
    return jnp.exp(s01) - jnp.exp(s23) + lam_init


LOG2E = math.log2(math.e)
DIFF_TRIP_WIDTHS = (4, 2, 1)


def _diff_attn_body(q_ref, k_ref, v_ref, g_ref, bt_ref, subln_ref, lamc_ref, o_ref, kb_scr, vb_scr,
                    *, tq, n_const, lam_init):
    qi = pl.program_id(2)
    dv = v_ref.shape[-1]
    tk = 2 * tq

    @pl.when(qi == 0)
    def _():
        kb_scr[...] = k_ref[0].astype(BF16)
        vb_scr[:, 0:dv] = v_ref[0].astype(BF16)
        lane = lax.broadcasted_iota(jnp.int32, (vb_scr.shape[0], dv), 1)
        vb_scr[:, dv:2 * dv] = jnp.where(lane == 0, 1.0, 0.0).astype(BF16)

    q = q_ref[0] * (DH_C ** -0.5 * LOG2E)
    lane = lax.broadcasted_iota(jnp.int32, q.shape, 1)
    qs = (jnp.where(lane < DH_C, q, 0.0).astype(BF16), jnp.where(lane >= DH_C, q, 0.0).astype(BF16))
    n_chunks = qi // 2 + 1

    def scores(kj):
        kb = kb_scr[pl.ds(pl.multiple_of(kj * tk, tk), tk), :]
        d = qi - 2 * kj
        ia, ib = jnp.minimum(d, n_const) + 1, jnp.minimum(d - 1, n_const) + 1
        out = []
        for c in range(2):
            s = _dot_nt(qs[c], kb)
            out.append((s[:, 0:tq] + bt_ref[c, ia], s[:, tq:tk] + bt_ref[c, ib]))
        return tuple(out)

    def softmax(sc, ms):
        out = []
        for (sa, sb), m in zip(sc, ms):
            m_new = jnp.maximum(m, jnp.max(jnp.maximum(sa, sb), axis=-1, keepdims=True))
            out.append((m_new, jnp.exp2(m - m_new),
                        jnp.exp2(sa - m_new).astype(BF16), jnp.exp2(sb - m_new).astype(BF16)))
        return tuple(out)

    def accumulate(kj, probs, accs):
        vb = vb_scr[pl.ds(pl.multiple_of(kj * tk, tk), tk), :]
        return tuple(alpha * acc + _dot(pa, vb[0:tq]) + _dot(pb, vb[tq:tk])
                     for (_, alpha, pa, pb), acc in zip(probs, accs))

    def trips(width, first, count, carry):
        def trip(i, carry):
            kjs = [first + i * width + u for u in range(width)]
            sc = [scores(kj) for kj in kjs]
            ms, accs = [c[0] for c in carry], [c[1] for c in carry]
            for kj, s in zip(kjs, sc):
                probs = softmax(s, ms)
                ms = [p[0] for p in probs]
                accs = accumulate(kj, probs, accs)
            return tuple(zip(ms, accs))

        return lax.fori_loop(0, count, trip, carry), first + count * width

    init = (jnp.full((tq, 1), NEG, F32), jnp.zeros((tq, 2 * dv), F32))
    carry, done = (init, init), 0
    for width in DIFF_TRIP_WIDTHS:
        carry, done = trips(width, done, (n_chunks - done) // width, carry)
    (_, acc0), (_, acc1) = carry
    lam = _lam_from(lamc_ref[...], lam_init)
    o = acc0[:, 0:dv] / acc0[:, dv:dv + 1] - lam * (acc1[:, 0:dv] / acc1[:, dv:dv + 1])
    oc = _rms(o, subln_ref[...]) * (1.0 - lam_init)
    o_ref[0] = oc * _silu(g_ref[0])


def _diff_bias_tiles(rel_bias, tq, n_const, n_heads):
    dist = (np.arange(-1, n_const + 1)[:, None, None] * tq + np.arange(tq)[None, :, None]
            - np.arange(tq)[None, None, :]).reshape((n_const + 2) * tq, tq)
    tiles = _table_by_column(rel_bias, _bucket_code(np.clip(dist, 0, None), dist >= 0))
    return tiles.reshape(2, n_heads, n_const + 2, tq, tq)


def _const_tile_index(tq):
    exact = N_BUCKETS // 2
    sat = math.ceil(exact * (T5_MAX_DIST / exact) ** ((N_BUCKETS - 1 - exact) / (N_BUCKETS - exact))) + 1
    return -(-(sat + tq - 1) // tq)


def _diff_attn_prompt(q, k, v, g, rel_bias, subln, lamc, lam_init, tq):
    b, s, w = q.shape
    hw = 2 * DH_C
    n_heads = w // hw
    n_const = _const_tile_index(tq)
    assert s % (2 * tq) == 0
    bias2 = rel_bias * LOG2E
    tiles = _diff_bias_tiles(bias2, tq, n_const, n_heads)
    qspec = pl.BlockSpec((1, tq, hw), lambda i, h, j: (i, j, h))
    kvspec = pl.BlockSpec((1, s, hw), lambda i, h, j: (i, 0, h))
    return pl.pallas_call(
        functools.partial(_diff_attn_body, tq=tq, n_const=n_const, lam_init=lam_init),
        grid=(b, n_heads, s // tq),
        in_specs=[qspec, kvspec, kvspec, qspec,
                  pl.BlockSpec((2, None, n_const + 2, tq, tq), lambda i, h, j: (0, h, 0, 0, 0)),
                  pl.BlockSpec((1, hw), lambda i, h, j: (0, 0)),
                  pl.BlockSpec((4, DH_C), lambda i, h, j: (0, 0))],
        out_specs=qspec,
        out_shape=jax.ShapeDtypeStruct((b, s, w), F32),
        scratch_shapes=[pltpu.VMEM((s, hw), BF16), pltpu.VMEM((s, 2 * hw), BF16)],
        compiler_params=_cparams(("parallel", "parallel", "arbitrary"), 56),
        name="diff_attn_prompt",
    )(q, k, v, g, tiles, subln.reshape(1, hw), lamc)


def _hgrn2_gates(q, f_logit, lb, live):
    f = lb + (1.0 - lb) * _sigmoid(f_logit)
    log_f, kk = jnp.log(f), 1.0 - f
    if live is not None:
        log_f, kk = jnp.where(live, log_f, 0.0), jnp.where(live, kk, 0.0)
    return _silu(q) * (DK_D ** -0.5), kk, log_f


def _hgrn2_body(q_ref, f_ref, v_ref, g_ref, lb_ref, gn_ref, s0_ref, y_ref, sl_ref, st_scr, o_scr,
                *, tb, chunk, n_heads):
    j = pl.program_id(1)

    @pl.when(j == 0)
    def _():
        for h in range(n_heads):
            st_scr[h] = s0_ref[0, h].T

    row = lax.broadcasted_iota(jnp.int32, (tb, DK_D), 0)
    pos = row % chunk
    r2 = lax.broadcasted_iota(jnp.int32, (tb, tb), 0)
    c2 = lax.broadcasted_iota(jnp.int32, (tb, tb), 1)
    same_chunk_causal = (r2 // chunk == c2 // chunk) & (c2 <= r2)
    heads = []
    for h in range(n_heads):
        ln = slice(h * DK_D, (h + 1) * DK_D)
        qf, kk, log_f = _hgrn2_gates(q_ref[0, :, ln], f_ref[0, :, ln], lb_ref[:, ln], None)
        v = v_ref[0, :, ln].astype(BF16)
        bc = log_f
        tot = log_f
        s = 1
        while s < chunk:
            bc = jnp.where(pos >= s, bc + pltpu.roll(bc, s, 0), bc)
            tot = tot + jnp.where((pos & s) == 0, pltpu.roll(tot, tb - s, 0), pltpu.roll(tot, s, 0))
            s *= 2
        qd = (qf * jnp.exp(bc)).astype(BF16)
        kd = (kk * jnp.exp(-bc)).astype(BF16)
        kl = (kk * jnp.exp(tot - bc)).astype(BF16)
        att = jnp.where(same_chunk_causal, _dot_nt(qd, kd), 0.0)
        o_scr[:, ln] = _dot(att.astype(BF16), v)
        heads.append((ln, qd, kl, v, jnp.exp(tot), st_scr[h]))

    for n in range(tb // chunk):
        rows = slice(n * chunk, (n + 1) * chunk)
        for h, (ln, qd, kl, v, gdec, st) in enumerate(heads):
            o_scr[rows, ln] += _dot_nt(qd[rows], st.astype(BF16))
            heads[h] = (ln, qd, kl, v, gdec, st * gdec[n * chunk:n * chunk + 1, :] + _dot_tn(v[rows], kl[rows]))

    for h, (ln, _, _, _, _, st) in enumerate(heads):
        st_scr[h] = st
        y_ref[0, :, ln] = _rms(o_scr[:, ln], gn_ref[...]) * _silu(g_ref[0, :, ln])

    @pl.when(j == pl.num_programs(1) - 1)
    def _():
        for h in range(n_heads):
            sl_ref[0, h] = st_scr[h].T


def _hgrn2_prompt(q, f, v, g, lb, gnorm, s0, tb):
    b, s, w = q.shape
    n_heads = w // DK_D
    seq = pl.BlockSpec((1, tb, w), lambda i, j: (i, j, 0))
    st = pl.BlockSpec((1, n_heads, DK_D, DV_D), lambda i, j: (i, 0, 0, 0))
    return pl.pallas_call(
        functools.partial(_hgrn2_body, tb=tb, chunk=CHUNK_D, n_heads=n_heads),
        grid=(b, s // tb),
        in_specs=[seq, seq, seq, seq,
                  pl.BlockSpec((1, w), lambda i, j: (0, 0)),
                  pl.BlockSpec((1, DV_D), lambda i, j: (0, 0)),
                  st],
        out_specs=[seq, st],
        out_shape=[jax.ShapeDtypeStruct((b, s, w), F32),
                   jax.ShapeDtypeStruct((b, n_heads, DK_D, DV_D), F32)],
        scratch_shapes=[pltpu.VMEM((n_heads, DV_D, DK_D), F32), pltpu.VMEM((tb, w), F32)],
        compiler_params=_cparams(("parallel", "arbitrary"), 40),
        name="hgrn2_prompt",
    )(q, f, v, g, lb.reshape(1, w), gnorm.reshape(1, DV_D), s0)


ROWS8 = 8


def _dil_decode_body(q_ref, kn_ref, vn_ref, gb_ref, kt_ref, vt_ref, tab_ref, tabn_ref, y_ref, *, n_heads):
    n_pat = len(DIL_PATTERNS)
    for h in range(n_heads):
        q = (q_ref[0, h] * (DH_B ** -0.5)).astype(BF16)
        s = _dot(q, kt_ref[0, h].astype(BF16))
        sn = _dot_nt(q, kn_ref[0, h].astype(BF16))
        parts = [s + tab_ref[h, g * ROWS8:(g + 1) * ROWS8] for g in range(n_pat)]
        parts += [sn + tabn_ref[h, g * ROWS8:(g + 1) * ROWS8, 0:ROWS8] for g in range(n_pat)]
        m = parts[0].max(axis=-1, keepdims=True)
        for p in parts[1:]:
            m = jnp.maximum(m, p.max(axis=-1, keepdims=True))
        ps = [jnp.exp(p - m) for p in parts]
        l = ps[0].sum(axis=-1, keepdims=True)
        for p in ps[1:]:
            l = l + p.sum(axis=-1, keepdims=True)
        pw = (ps[0] + ps[1] + ps[2]).astype(BF16)
        pn = (ps[3] + ps[4] + ps[5]).astype(BF16)
        o = _dot_nt(pw, vt_ref[0, h].astype(BF16)) + _dot(pn, vn_ref[0, h].astype(BF16))
        y_ref[0, h] = (o / l) * _silu(gb_ref[0, h])


def _dil_decode(q, k, v, gb, win_k, win_v, rel_bias):
    b, nt, n_heads, dh = q.shape
    l_buf = win_k.shape[1]
    assert nt <= ROWS8 and rel_bias.shape[1] == n_heads
    tt = np.tile(np.arange(ROWS8), len(DIL_PATTERNS))[:, None]
    dil = np.repeat([d for _, d in DIL_PATTERNS], ROWS8)[:, None]
    live = tt < nt

    def code(dd, lo):
        return _bucket_code(np.clip(dd, 0, None), live & (dd % dil == 0) & (dd >= lo * dil) & (dd <= BAND * dil))

    tab = _table_by_column(rel_bias, code(l_buf + tt - np.arange(l_buf)[None, :], 1))
    tn = np.arange(128)[None, :]
    tabn = _table_by_column(rel_bias, np.where(tn < nt, code(tt - tn, 0), -1).astype(np.int32))

    by_head = lambda z: jnp.pad(jnp.transpose(z, (0, 2, 1, 3)), ((0, 0), (0, 0), (0, ROWS8 - nt), (0, 0)))
    pos_minor = lambda z: jnp.transpose(z, (0, 2, 3, 1))
    new = pl.BlockSpec((1, n_heads, ROWS8, dh), lambda i: (i, 0, 0, 0))
    win = pl.BlockSpec((1, n_heads, dh, l_buf), lambda i: (i, 0, 0, 0))
    const = lambda a: pl.BlockSpec(a.shape, lambda i: (0,) * a.ndim)
    y = pl.pallas_call(
        functools.partial(_dil_decode_body, n_heads=n_heads),
        grid=(b,),
        in_specs=[new, new, new, new, win, win, const(tab), const(tabn)],
        out_specs=new,
        out_shape=jax.ShapeDtypeStruct((b, n_heads, ROWS8, dh), F32),
        compiler_params=_cparams(("parallel",), 40),
        name="dil_decode",
    )(by_head(q), by_head(k), by_head(v), by_head(gb), pos_minor(win_k), pos_minor(win_v), tab, tabn)
    return jnp.transpose(y[:, :, :nt], (0, 2, 1, 3))


def _diff_decode_body(pt_ref, qm_ref, kn_ref, vn_ref, g_ref, tab_ref, tabn_ref, subln_ref, lamc_ref, *rest,
                      nt, n_heads, n_pages, lam_init):
    del pt_ref
    k_pages, v_pages, y_ref = rest[:n_pages], rest[n_pages:2 * n_pages], rest[2 * n_pages]
    qm = (qm_ref[0] * (DH_C ** -0.5)).astype(BF16)
    rows, hw = qm.shape
    grp = 2 * n_heads
    pw = PAGE * n_heads
    ss = [_dot_nt(qm, k_pages[p][0].astype(BF16)) + tab_ref[:, p * pw:(p + 1) * pw] for p in range(n_pages)]
    sn = _dot_nt(qm, kn_ref[0].astype(BF16)) + tabn_ref[:, 0:kn_ref.shape[1]]
    m = sn.max(axis=-1, keepdims=True)
    for s in ss:
        m = jnp.maximum(m, s.max(axis=-1, keepdims=True))
    ps = [jnp.exp(s - m) for s in ss]
    pn = jnp.exp(sn - m)
    l = pn.sum(axis=-1, keepdims=True)
    for p in ps:
        l = l + p.sum(axis=-1, keepdims=True)
    lam = _lam_from(lamc_ref[...], lam_init)
    row_map = (lax.broadcasted_iota(jnp.int32, (rows, 1), 0) % grp) // n_heads
    coef = jnp.where(row_map == 0, 1.0, -lam) / l
    o = _dot((pn * coef).astype(BF16), vn_ref[0].astype(BF16))
    for p in range(n_pages):
        o = o + _dot((ps[p] * coef).astype(BF16), v_pages[p][0].astype(BF16))
    o = o.reshape(nt, grp, hw)
    oc = o[:, 0:n_heads, :] + o[:, n_heads:grp, :]
    y_ref[0] = _rms(oc, subln_ref[...]) * (1.0 - lam_init) * _silu(g_ref[0])


def _diff_decode(q, k, v, g, cache_k, cache_v, page_table, rel_bias, subln, lamc, lam_init):
    b, nt, w = q.shape
    hw = 2 * DH_C
    n_heads = w // hw
    n_phys = cache_k.shape[0]
    n_pages = page_table.shape[1]
    past = n_pages * PAGE
    grp = 2 * n_heads
    rows = nt * grp
    assert rel_bias.shape[1] == grp
    tt = np.arange(rows)[:, None] // grp
    hq = np.arange(rows)[:, None] % n_heads
    pos = lambda n: np.arange(n)[None, :] // n_heads
    same = lambda n: np.arange(n)[None, :] % n_heads == hq
    tab = _table_by_row(rel_bias, _bucket_code(past + tt - pos(past * n_heads), same(past * n_heads)))
    jn = tt - pos(128)
    tabn = _table_by_row(rel_bias, _bucket_code(np.clip(jn, 0, None), same(128) & (jn >= 0) & (pos(128) < nt)))

    map_mask = (np.arange(hw)[None, :] // DH_C == np.arange(2)[:, None]).astype(np.float32)
    qm = q.reshape(b, nt, 1, n_heads, hw) * map_mask[None, None, :, None, :]
    by_head = lambda z: z.reshape(b, nt * n_heads, hw)

    per_b = lambda *shape: pl.BlockSpec((1,) + shape, lambda i, pt: (i,) + (0,) * len(shape))
    const = lambda *shape: pl.BlockSpec(shape, lambda i, pt: (0,) * len(shape))
    page = lambda p: pl.BlockSpec((1, PAGE * n_heads, hw), lambda i, pt, p=p: (pt[i * n_pages + p], 0, 0))
    grid_spec = pltpu.PrefetchScalarGridSpec(
        num_scalar_prefetch=1,
        grid=(b,),
        in_specs=[per_b(rows, hw), per_b(nt * n_heads, hw), per_b(nt * n_heads, hw), per_b(nt, n_heads, hw),
                  const(rows, past * n_heads), const(rows, 128), const(1, hw), const(4, DH_C)]
                 + [page(p) for p in range(n_pages)] * 2,
        out_specs=per_b(nt, n_heads, hw),
    )
    rows_of = lambda c: c.reshape(n_phys, PAGE * n_heads, hw)
    return pl.pallas_call(
        functools.partial(_diff_decode_body, nt=nt, n_heads=n_heads, n_pages=n_pages, lam_init=lam_init),
        grid_spec=grid_spec,
        out_shape=jax.ShapeDtypeStruct((b, nt, n_heads, hw), F32),
        compiler_params=_cparams(("arbitrary",), 40),
        name="diff_decode",
    )(page_table.reshape(-1), qm.reshape(b, rows, hw), by_head(k), by_head(v), g.reshape(b, nt, n_heads, hw),
      tab, tabn, subln.reshape(1, hw), lamc, *([rows_of(cache_k)] * n_pages), *([rows_of(cache_v)] * n_pages))


def _hgrn2_decode_body(q_ref, f_ref, ft_ref, v_ref, g_ref, lb_ref, lbc_ref, gn_ref, s0_ref, y_ref, sl_ref,
                       *, nt, n_heads):
    row = lax.broadcasted_iota(jnp.int32, (ROWS8, DK_D), 0)
    r2 = lax.broadcasted_iota(jnp.int32, (ROWS8, ROWS8), 0)
    c2 = lax.broadcasted_iota(jnp.int32, (ROWS8, ROWS8), 1)
    for h in range(n_heads):
        ln = slice(h * DK_D, (h + 1) * DK_D)
        qf, kk, log_f = _hgrn2_gates(q_ref[0, :, ln], f_ref[0, :, ln], lb_ref[:, ln], row < nt)
        v = v_ref[0, :, ln].astype(BF16)
        bc = log_f
        s = 1
        while s < ROWS8:
            bc = jnp.where(row >= s, bc + pltpu.roll(bc, s, 0), bc)
            s *= 2
        tot = bc[ROWS8 - 1:ROWS8, :]
        qd = (qf * jnp.exp(bc)).astype(BF16)
        kd = (kk * jnp.exp(-bc)).astype(BF16)
        kl = (kk * jnp.exp(tot - bc)).astype(BF16)
        s0 = s0_ref[0, h]
        att = jnp.where(c2 <= r2, _dot_nt(qd, kd), 0.0)
        o = _dot(att.astype(BF16), v) + _dot(qd, s0.astype(BF16))
        lbc = lbc_ref[h]
        fc = lbc + (1.0 - lbc) * _sigmoid(ft_ref[0, h])
        decay = fc[:, 0:1]
        for t in range(1, nt):
            decay = decay * fc[:, t:t + 1]
        sl_ref[0, h] = decay * s0 + _dot_tn(kl, v)
        y_ref[0, :, ln] = _rms(o, gn_ref[...]) * _silu(g_ref[0, :, ln])


def _hgrn2_decode(q, f, v, g, lb, gnorm, s0):
    b, nt, w = q.shape
    n_heads = w // DK_D
    assert nt <= ROWS8
    pad = lambda z: jnp.pad(z, ((0, 0), (0, ROWS8 - nt), (0, 0)))
    cols = jnp.transpose(f.reshape(b, nt, n_heads, DK_D), (0, 2, 3, 1))
    per_b = lambda *shape: pl.BlockSpec((1,) + shape, lambda i: (i,) + (0,) * len(shape))
    const = lambda *shape: pl.BlockSpec(shape, lambda i: (0,) * len(shape))
    y, s_last = pl.pallas_call(
        functools.partial(_hgrn2_decode_body, nt=nt, n_heads=n_heads),
        grid=(b,),
        in_specs=[per_b(ROWS8, w), per_b(ROWS8, w), per_b(n_heads, DK_D, nt), per_b(ROWS8, w), per_b(ROWS8, w),
                  const(1, w), const(n_heads, DK_D, 1), const(1, DV_D), per_b(n_heads, DK_D, DV_D)],
        out_specs=[per_b(ROWS8, w), per_b(n_heads, DK_D, DV_D)],
        out_shape=[jax.ShapeDtypeStruct((b, ROWS8, w), F32),
                   jax.ShapeDtypeStruct((b, n_heads, DK_D, DV_D), F32)],
        compiler_params=_cparams(("parallel",), 32),
        name="hgrn2_decode",
    )(pad(q), pad(f), cols, pad(v), pad(g), lb.reshape(1, w), lb.reshape(n_heads, DK_D, 1), gnorm.reshape(1, DV_D),
      s0)
    return y[:, :nt], s_last


def _layer_ab(x, group, w_in, gate_params, w_out, rel_bias, state):
    b, t, d = x.shape
    c = w_out.shape[0] // 2
    xa, ga, q, k, v, gb = [z.reshape(b, t, c) for z in _norm_proj(x.reshape(b * t, d), *group, w_in, (c,) * 6,
                                                                    min(512, b * t))]
    by_head = lambda z: z.reshape(z.shape[0], z.shape[1], c // DH_B, DH_B)
    if state is None:
        ya, conv_new, h_last = _rglru_prompt(xa, ga, jnp.zeros((b, CONV_W - 1, c), F32), jnp.zeros((b, c), F32),
                                             gate_params, tb=256)
        h_last = h_last.reshape(b, c)
        yb = _dil_prompt(q, k, v, gb, rel_bias)
        keep = min(WIN_MAX, t)
        k_rows, v_rows = by_head(k[:, t - keep:]), by_head(v[:, t - keep:])
    else:
        conv_buf, h0, win_k, win_v = state
        ya, conv_new, h_last = _rglru_decode(xa, ga, conv_buf, h0, gate_params)
        k_rows, v_rows = by_head(k), by_head(v)
        yb = _dil_decode(by_head(q), k_rows, v_rows, by_head(gb), win_k, win_v, rel_bias)
    x = _out_proj(x.reshape(b * t, d), ya.reshape(b * t, c), yb.reshape(b * t, c), w_out, None, min(512, b * t))
    return x.reshape(b, t, d), conv_new, h_last, k_rows, v_rows


def _layer_cd(x, group, w_in, lamc, subln, lb, gnorm, w_out, rel_bias, lam_init, final_g, state):
    b, t, d = x.shape
    c = w_out.shape[0] // 2
    qc, kc, vc, gc, qd, fd, idd, gd = [z.reshape(b, t, c) for z in
                                       _norm_proj(x.reshape(b * t, d), *group, w_in, (c,) * 8, min(512, b * t))]
    if state is None:
        yc = _diff_attn_prompt(qc, kc, vc, gc, rel_bias, subln, lamc, lam_init, tq=256)
        yd, s_last = _hgrn2_prompt(qd, fd, idd, gd, lb, gnorm, jnp.zeros((b, c // DK_D, DK_D, DV_D), F32), tb=256)
    else:
        s0, cache_k, cache_v, page_table = state
        yc = _diff_decode(qc, kc, vc, gc, cache_k, cache_v, page_table, rel_bias, subln, lamc, lam_init)
        yd, s_last = _hgrn2_decode(qd, fd, idd, gd, lb, gnorm, s0)
    y = _out_proj(x.reshape(b * t, d), yc.reshape(b * t, c), yd.reshape(b * t, c), w_out, final_g,
                  min(512, b * t))
    heads = c // (2 * DH_C)
    return (y.reshape(b, t, d), kc.reshape(b, t, heads, 2 * DH_C), vc.reshape(b, t, heads, 2 * DH_C), s_last)


def kernel(x_prompt, x_sample, state_conv_a, state_h_a, cache_win_k, cache_win_v, cache_k_c, cache_v_c, state_s_d,
           page_table, norm_g, norm_final, rel_bias, w_in_ab, conv_w_a, conv_b_a, w_r_a, b_r_a, w_i_a, b_i_a, lam_a,
           w_out_ab, w_in_cd, lam_c, subln_c, lb_d, gnorm_d, w_out_cd):
    depth = norm_g.shape[0]
    assert depth == 2
    lb_soft = jax.nn.softmax(lb_d.astype(F32), axis=0)
    lb_all = jnp.cumsum(lb_soft, axis=0) - lb_soft[0]

    gate_params = _gate_params(conv_w_a[0], conv_b_a[0], w_r_a[0], b_r_a[0], w_i_a[0], b_i_a[0], lam_a[0])
    xp, conv_p, h_p, wk_p, wv_p = _layer_ab(x_prompt, (norm_g[0],), w_in_ab[0], gate_params, w_out_ab[0], rel_bias,
                                            None)
    xs, conv_s, h_s, wk_s, wv_s = _layer_ab(x_sample, (norm_g[0],), w_in_ab[0], gate_params, w_out_ab[0], rel_bias,
                                            (state_conv_a[0], state_h_a[0], cache_win_k[0], cache_win_v[0]))
    lam_init = 0.8 - 0.6 * math.exp(-0.3 * 1)
    cd = (w_in_cd[0], lam_c[0], subln_c[0], lb_all[1], gnorm_d[0], w_out_cd[0], rel_bias, lam_init, norm_final)
    yp, kc_p, vc_p, s_p = _layer_cd(xp, (norm_g[1],), *cd, None)
    ys, kc_s, vc_s, s_s = _layer_cd(xs, (norm_g[1],), *cd, (state_s_d[0], cache_k_c[0], cache_v_c[0], page_table))
    e = lambda z: z[None]
    return (yp, ys, e(conv_p), e(h_p), e(wk_p), e(wv_p), e(kc_p), e(vc_p), e(s_p),
            e(conv_s), e(h_s), e(wk_s), e(wv_s), e(kc_s), e(vc_s), e(s_s))
```

```python
import functools
import math

import numpy as np
import jax
import jax.numpy as jnp
from jax import lax
from jax.experimental import pallas as pl
from jax.experimental.pallas import tpu as pltpu

F32 = jnp.float32
BF16 = jnp.bfloat16

N_BLK_A = 8
CONV_W = 4
RG_C = 8.0
DH_B = 64
DIL_PATTERNS = ((128, 1), (512, 4), (2048, 16))
BAND = 128
WIN_MAX = 2048
DH_C = 64
DK_D = 128
DV_D = 128
CHUNK_D = 16
N_BUCKETS = 32
T5_MAX_DIST = 2048
PAGE = 128
EPS = 1e-6
NEG = -1e30
MIB = 2 ** 20


def _cparams(sem, vmem_mib):
    return pltpu.CompilerParams(dimension_semantics=sem, vmem_limit_bytes=vmem_mib * MIB)


def _dot(a, b):
    return jnp.dot(a, b, preferred_element_type=F32)


def _dot_nt(a, b):
    return lax.dot_general(a, b, (((1,), (1,)), ((), ())), preferred_element_type=F32)


def _dot_tn(a, b):
    return lax.dot_general(a, b, (((0,), (0,)), ((), ())), preferred_element_type=F32)


def _sigmoid(x):
    return 1.0 / (1.0 + jnp.exp(-x))


def _silu(x):
    return x * _sigmoid(x)


def _rms(x, g):
    return x * lax.rsqrt(jnp.mean(x * x, axis=-1, keepdims=True) + EPS) * g


def _t5_bucket(dist):
    n = np.maximum(np.asarray(dist), 0)
    exact = N_BUCKETS // 2
    nf = np.maximum(n, 1).astype(np.float32)
    val = np.log(nf / np.float32(exact)) / np.float32(math.log(T5_MAX_DIST / exact)) * np.float32(N_BUCKETS - exact)
    large = np.minimum(exact + val.astype(np.int32), N_BUCKETS - 1)
    return np.where(n < exact, n, large).astype(np.int32)


def _norm_proj_body(x_ref, g_ref, w_ref, *o_refs):
    hn = _rms(x_ref[...], g_ref[...]).astype(BF16)
    off = 0
    for o_ref in o_refs:
        w = o_ref.shape[-1]
        o_ref[...] = _dot(hn, w_ref[:, off:off + w])
        off += w


def _norm_proj(x, g, w, widths, tm):
    t, d = x.shape
    n = w.shape[1]
    return pl.pallas_call(
        _norm_proj_body,
        grid=(t // tm,),
        in_specs=[pl.BlockSpec((tm, d), lambda i: (i, 0)),
                  pl.BlockSpec((1, d), lambda i: (0, 0)),
                  pl.BlockSpec((d, n), lambda i: (0, 0))],
        out_specs=[pl.BlockSpec((tm, wd), lambda i: (i, 0)) for wd in widths],
        out_shape=[jax.ShapeDtypeStruct((t, wd), F32) for wd in widths],
        compiler_params=_cparams(("parallel",), 56),
        name="norm_proj",
    )(x, g.reshape(1, d), w.astype(BF16))


def _rglru_gates(xc, wr, br, wi, bi, lam):
    xcb = xc.astype(BF16)
    r = _sigmoid(_dot(xcb, wr) + br)
    gi = _sigmoid(_dot(xcb, wi) + bi)
    nl = -lam
    softplus = jnp.maximum(nl, 0.0) + jnp.log(1.0 + jnp.exp(-jnp.abs(nl)))
    log_a = (-RG_C) * r * softplus
    a = jnp.exp(log_a)
    u = jnp.sqrt(1.0 - jnp.exp(2.0 * log_a)) * (gi * xc)
    return a, u


def _rglru_prompt_body(xa_ref, ga_ref, cb_ref, h0_ref, cw_ref, cbias_ref, wr_ref, br_ref, wi_ref, bi_ref,
                       lam_ref, ya_ref, cnew_ref, hl_ref, xp_scr, h_scr, *, tb):
    j = pl.program_id(1)

    @pl.when(j == 0)
    def _():
        xp_scr[5:8, :] = cb_ref[0]
        h_scr[...] = h0_ref[0]

    xp_scr[8:8 + tb, :] = xa_ref[0]
    cw = cw_ref[...]
    xc = cbias_ref[...] + xp_scr[5:5 + tb, :] * cw[0:1, :]
    for k in range(1, CONV_W):
        xc = xc + xp_scr[5 + k:5 + k + tb, :] * cw[k:k + 1, :]
    tail = xp_scr[5 + tb:8 + tb, :]
    cnew_ref[0] = tail
    xp_scr[5:8, :] = tail

    a, u = _rglru_gates(xc, wr_ref[...], br_ref[...], wi_ref[...], bi_ref[...], lam_ref[...])
    row = lax.broadcasted_iota(jnp.int32, a.shape, 0)
    s = 1
    while s < tb:
        keep = row >= s
        u = jnp.where(keep, a * pltpu.roll(u, s, 0) + u, u)
        a = jnp.where(keep, a * pltpu.roll(a, s, 0), a)
        s *= 2
    h = a * h_scr[...] + u
    h_last = h[tb - 1:tb, :]
    h_scr[...] = h_last
    hl_ref[0] = h_last
    ya_ref[0] = h * _silu(ga_ref[0])


def _rglru_decode_body(xa_ref, ga_ref, cb_ref, h0_ref, cw_ref, cbias_ref, wr_ref, br_ref, wi_ref, bi_ref,
                       lam_ref, ya_ref, cnew_ref, hl_ref):
    nt = xa_ref.shape[0]
    xp = [cb_ref[i] for i in range(CONV_W - 1)] + [xa_ref[t] for t in range(nt)]
    cw = cw_ref[...]
    h = h0_ref[...]
    for t in range(nt):
        xc = cbias_ref[...] + xp[t] * cw[0:1, :]
        for k in range(1, CONV_W):
            xc = xc + xp[t + k] * cw[k:k + 1, :]
        a, u = _rglru_gates(xc, wr_ref[...], br_ref[...], wi_ref[...], bi_ref[...], lam_ref[...])
        h = a * h + u
        ya_ref[t] = h * _silu(ga_ref[t])
    for i in range(CONV_W - 1):
        cnew_ref[i] = xp[nt + i]
    hl_ref[...] = h


def _block_diag(w):
    g, bi, bj = w.shape
    eye = jnp.eye(g, dtype=w.dtype)
    return (w[:, :, None, :] * eye[:, None, :, None]).reshape(g * bi, g * bj)


def _gate_params(conv_w, conv_b, w_r, b_r, w_i, b_i, lam):
    c = conv_w.shape[-1]
    return (conv_w, conv_b.reshape(1, c), _block_diag(w_r).astype(BF16), b_r.reshape(1, c),
            _block_diag(w_i).astype(BF16), b_i.reshape(1, c), lam.reshape(1, c))


def _rglru_prompt(xa, ga, conv_buf, h0, gate_params, tb):
    b, t, c = xa.shape
    full = lambda shape: pl.BlockSpec(shape, lambda i, j: (0,) * len(shape))
    seq = pl.BlockSpec((1, tb, c), lambda i, j: (i, j, 0))
    return pl.pallas_call(
        functools.partial(_rglru_prompt_body, tb=tb),
        grid=(b, t // tb),
        in_specs=[seq, seq,
                  pl.BlockSpec((1, CONV_W - 1, c), lambda i, j: (i, 0, 0)),
                  pl.BlockSpec((1, 1, c), lambda i, j: (i, 0, 0)),
                  full((CONV_W, c)), full((1, c)), full((c, c)), full((1, c)), full((c, c)), full((1, c)),
                  full((1, c))],
        out_specs=[seq,
                   pl.BlockSpec((1, CONV_W - 1, c), lambda i, j: (i, 0, 0)),
                   pl.BlockSpec((1, 1, c), lambda i, j: (i, 0, 0))],
        out_shape=[jax.ShapeDtypeStruct((b, t, c), F32),
                   jax.ShapeDtypeStruct((b, CONV_W - 1, c), F32),
                   jax.ShapeDtypeStruct((b, 1, c), F32)],
        scratch_shapes=[pltpu.VMEM((tb + 8, c), F32), pltpu.VMEM((1, c), F32)],
        compiler_params=_cparams(("parallel", "arbitrary"), 40),
        name="rglru_prompt",
    )(xa, ga, conv_buf, h0.reshape(b, 1, c), *gate_params)


def _rglru_decode(xa, ga, conv_buf, h0, gate_params):
    b, t, c = xa.shape
    tm = lambda z: jnp.transpose(z, (1, 0, 2))
    ya, cnew, hl = pl.pallas_call(
        _rglru_decode_body,
        out_shape=[jax.ShapeDtypeStruct((t, b, c), F32),
                   jax.ShapeDtypeStruct((CONV_W - 1, b, c), F32),
                   jax.ShapeDtypeStruct((b, c), F32)],
        compiler_params=pltpu.CompilerParams(vmem_limit_bytes=40 * MIB),
        name="rglru_decode",
    )(tm(xa), tm(ga), tm(conv_buf), h0, *gate_params)
    return tm(ya), tm(cnew), hl


TABLE_VREGS = 16


def _lane_chunk(x):
    return next(c for c in (2048, 1024, 512, 256, 128) if x % c == 0)


def _table_by_column_body(vals_ref, code_ref, out_ref, *, rc):
    base = pl.program_id(0) * N_BUCKETS

    def chunk(ci, carry):
        r0 = pl.multiple_of(ci * rc, rc)
        code = code_ref[pl.ds(r0, rc), :]
        look = lambda k, out: jnp.where(code == k, vals_ref[base + k], out)
        out_ref[0, pl.ds(r0, rc), :] = lax.fori_loop(0, N_BUCKETS, look, jnp.full(code.shape, NEG, F32))
        return carry

    lax.fori_loop(0, code_ref.shape[0] // rc, chunk, 0)


def _table_by_column(rel_bias, code):
    r, x = code.shape
    n_col = rel_bias.shape[1]
    rc = min(r, max(8, TABLE_VREGS * 1024 // x))
    assert r % rc == 0
    return pl.pallas_call(
        functools.partial(_table_by_column_body, rc=rc),
        grid=(n_col,),
        in_specs=[pl.BlockSpec(memory_space=pltpu.SMEM), pl.BlockSpec((r, x), lambda g: (0, 0))],
        out_specs=pl.BlockSpec((1, r, x), lambda g: (g, 0, 0)),
        out_shape=jax.ShapeDtypeStruct((n_col, r, x), F32),
        compiler_params=_cparams(("parallel",), 32),
        name="bias_table_by_column",
    )(rel_bias.T.reshape(-1), code)


def _table_by_row_body(rbt_ref, code_ref, out_ref, *, lc):
    rbt = rbt_ref[...]
    n_col = rbt.shape[0]

    def group(gi, carry):
        r0 = pl.multiple_of(gi * n_col, n_col)
        for x0 in range(0, code_ref.shape[1], lc):
            code = code_ref[pl.ds(r0, n_col), x0:x0 + lc]
            out = jnp.full(code.shape, NEG, F32)
            for k in range(N_BUCKETS):
                out = jnp.where(code == k, rbt[:, k:k + 1], out)
            out_ref[pl.ds(r0, n_col), x0:x0 + lc] = out
        return carry

    lax.fori_loop(0, code_ref.shape[0] // n_col, group, 0)


def _table_by_row(rel_bias, code):
    r, x = code.shape
    return pl.pallas_call(
        functools.partial(_table_by_row_body, lc=_lane_chunk(x)),
        out_shape=jax.ShapeDtypeStruct((r, x), F32),
        name="bias_table_by_row",
    )(rel_bias.T, code)


def _bucket_code(dist, valid):
    return np.where(valid, _t5_bucket(dist), -1).astype(np.int32)


DIL_GROUP = 4


def _rows(start, dil):
    return pl.ds(start, BAND, stride=dil) if dil > 1 else pl.ds(start, BAND)


def _dil_prompt_body(q_ref, k_ref, v_ref, gb_ref, bt_ref, y_ref, k_scr, v_scr, o_scr, lse_scr, *, unit, n_heads):
    n = pl.program_id(2)

    @pl.when(n == 0)
    def _():
        k_scr[0:unit, :] = jnp.zeros((unit, k_scr.shape[1]), F32)
        v_scr[0:unit, :] = jnp.zeros((unit, v_scr.shape[1]), F32)

    k_scr[unit:2 * unit, :] = k_ref[0]
    v_scr[unit:2 * unit, :] = v_ref[0]
    for g, (_, dil) in enumerate(DIL_PATTERNS):
        span = BAND * dil

        def blocks(ci, carry, g=g, dil=dil, span=span):
            lane_head = lax.broadcasted_iota(jnp.int32, (BAND, k_scr.shape[1]), 1) // DH_B
            ones = jnp.ones((BAND, k_scr.shape[1]), BF16)
            loaded = []
            for u in range(DIL_GROUP):
                c = ci * DIL_GROUP + u
                sub = c // dil
                base = sub * span + c % dil
                if dil == 1:
                    base = pl.multiple_of(base, BAND)
                loaded.append((base, jnp.logical_or(n > 0, sub > 0),
                               q_ref[0, _rows(base, dil), :] * (DH_B ** -0.5),
                               k_scr[_rows(unit + base, dil), :].astype(BF16),
                               k_scr[_rows(unit + base - span, dil), :].astype(BF16),
                               v_scr[_rows(unit + base, dil), :].astype(BF16),
                               v_scr[_rows(unit + base - span, dil), :].astype(BF16)))
            scores = []
            for _, has_prev, q, kc, kp, _, _ in loaded:
                for h in range(n_heads):
                    qh = jnp.where(lane_head == h, q, 0.0).astype(BF16)
                    scores.append((jnp.where(has_prev, _dot_nt(qh, kp) + bt_ref[h, g, :, 0:BAND], NEG),
                                   _dot_nt(qh, kc) + bt_ref[h, g, :, BAND:2 * BAND]))
            probs = []
            for sp, sc in scores:
                m = jnp.max(jnp.maximum(sp, sc), axis=-1, keepdims=True)
                probs.append((m, jnp.exp(sp - m).astype(BF16), jnp.exp(sc - m).astype(BF16)))
            for u, (base, _, _, _, _, vc, vp) in enumerate(loaded):
                o_sel = lse_sel = None
                for h in range(n_heads):
                    m, pp, pc = probs[u * n_heads + h]
                    l = _dot(pp, ones) + _dot(pc, ones)
                    o = (_dot(pp, vp) + _dot(pc, vc)) / l
                    lse = m + jnp.log(l)
                    o_sel = o if h == 0 else jnp.where(lane_head == h, o, o_sel)
                    lse_sel = lse if h == 0 else jnp.where(lane_head == h, lse, lse_sel)
                o_scr[g, _rows(base, dil), :] = o_sel
                lse_scr[g, _rows(base, dil), :] = lse_sel
            return carry

        lax.fori_loop(0, unit // BAND // DIL_GROUP, blocks, 0)

    lse1, lse2, lse3 = lse_scr[0], lse_scr[1], lse_scr[2]
    m = jnp.maximum(jnp.maximum(lse1, lse2), lse3)
    e1, e2, e3 = jnp.exp(lse1 - m), jnp.exp(lse2 - m), jnp.exp(lse3 - m)
    ob = (e1 * o_scr[0] + e2 * o_scr[1] + e3 * o_scr[2]) / (e1 + e2 + e3)
    y_ref[0] = ob * _silu(gb_ref[0])
    k_scr[0:unit, :] = k_ref[0]
    v_scr[0:unit, :] = v_ref[0]


def _dil_prompt(q, k, v, gb, rel_bias):
    b, s, w = q.shape
    unit = BAND * DIL_PATTERNS[-1][1]
    lw = 2 * DH_B
    n_heads = w // DH_B
    idx = np.arange(BAND)[:, None] + BAND - np.arange(2 * BAND)[None, :]
    ok = (idx >= 0) & (idx <= BAND)
    code = np.concatenate([_bucket_code(dil * np.clip(idx, 0, BAND), ok) for _, dil in DIL_PATTERNS])
    tiles = _table_by_column(rel_bias, code).reshape(n_heads, len(DIL_PATTERNS), BAND, 2 * BAND)
    seq = pl.BlockSpec((1, unit, lw), lambda i, hp, n: (i, n, hp))
    return pl.pallas_call(
        functools.partial(_dil_prompt_body, unit=unit, n_heads=lw // DH_B),
        grid=(b, w // lw, s // unit),
        in_specs=[seq, seq, seq, seq,
                  pl.BlockSpec((lw // DH_B, len(DIL_PATTERNS), BAND, 2 * BAND), lambda i, hp, n: (hp, 0, 0, 0))],
        out_specs=seq,
        out_shape=jax.ShapeDtypeStruct((b, s, w), F32),
        scratch_shapes=[pltpu.VMEM((2 * unit, lw), F32), pltpu.VMEM((2 * unit, lw), F32),
                        pltpu.VMEM((len(DIL_PATTERNS), unit, lw), F32),
                        pltpu.VMEM((len(DIL_PATTERNS), unit, lw), F32)],
        compiler_params=_cparams(("parallel", "parallel", "arbitrary"), 40),
        name="dil_prompt",
    )(q, k, v, gb, tiles)


def _out_proj_body(x_ref, ya_ref, yb_ref, wa_ref, wb_ref, *rest, final_norm):
    y = x_ref[...] + _dot(ya_ref[...].astype(BF16), wa_ref[...]) + _dot(yb_ref[...].astype(BF16), wb_ref[...])
    if final_norm:
        y = _rms(y, rest[0][...])
    rest[-1][...] = y


def _out_proj(x, ya, yb, w_out, final_g, tm):
    t, d = x.shape
    ca, cb = ya.shape[1], yb.shape[1]
    row = lambda c: pl.BlockSpec((tm, c), lambda i: (i, 0))
    const = lambda shape: pl.BlockSpec(shape, lambda i: (0, 0))
    wa = w_out[:ca].astype(BF16)
    wb = w_out[ca:].astype(BF16)
    args = [x, ya, yb, wa, wb]
    in_specs = [row(d), row(ca), row(cb), const((ca, d)), const((cb, d))]
    if final_g is not None:
        args.append(final_g.reshape(1, d))
        in_specs.append(const((1, d)))
    return pl.pallas_call(
        functools.partial(_out_proj_body, final_norm=final_g is not None),
        grid=(t // tm,),
        in_specs=in_specs,
        out_specs=row(d),
        out_shape=jax.ShapeDtypeStruct((t, d), F32),
        compiler_params=_cparams(("parallel",), 48),
        name="out_proj",
    )(*args)


def _lam_from(lamc, lam_init):
    s01 = jnp.sum(lamc[0:1, :] * lamc[1:2, :], axis=-1, keepdims=True)
    s23 = jnp.sum(lamc[2:3, :] * lamc[3:4, :], axis=-1, keepdims=True)
    return jnp.exp(s01) - jnp.exp(s23) + lam_init


LOG2E = math.log2(math.e)
DIFF_TRIP_WIDTHS = (4, 2, 1)


def _diff_attn_body(q_ref, k_ref, v_ref, g_ref, bt_ref, subln_ref, lamc_ref, o_ref, kb_scr, vb_scr,
                    *, tq, n_const, lam_init):
    qi = pl.program_id(2)
    dv = v_ref.shape[-1]
    tk = 2 * tq

    @pl.when(qi == 0)
    def _():
        kb_scr[...] = k_ref[0].astype(BF16)
        vb_scr[:, 0:dv] = v_ref[0].astype(BF16)
        lane = lax.broadcasted_iota(jnp.int32, (vb_scr.shape[0], dv), 1)
        vb_scr[:, dv:2 * dv] = jnp.where(lane == 0, 1.0, 0.0).astype(BF16)

    q = q_ref[0] * (DH_C ** -0.5 * LOG2E)
    lane = lax.broadcasted_iota(jnp.int32, q.shape, 1)
    qm = (jnp.where(lane < DH_C, q, 0.0).astype(BF16), jnp.where(lane >= DH_C, q, 0.0).astype(BF16))
    chains = [(r, c) for r in range(2) for c in range(2)]
    n_chunks = qi + 1

    def scores(kj):
        kb = kb_scr[pl.ds(pl.multiple_of(kj * tk, tk), tk), :]
        out = []
        for r, c in chains:
            d = 2 * (qi - kj) + r
            ia, ib = jnp.minimum(d, n_const) + 1, jnp.minimum(d - 1, n_const) + 1
            s = _dot_nt(qm[c][r * tq:(r + 1) * tq], kb)
            out.append((s[:, 0:tq] + bt_ref[c, ia], s[:, tq:tk] + bt_ref[c, ib]))
        return tuple(out)

    def softmax(sc, ms):
        out = []
        for (sa, sb), m in zip(sc, ms):
            m_new = jnp.maximum(m, jnp.max(jnp.maximum(sa, sb), axis=-1, keepdims=True))
            out.append((m_new, jnp.exp2(m - m_new),
                        jnp.exp2(sa - m_new).astype(BF16), jnp.exp2(sb - m_new).astype(BF16)))
        return tuple(out)

    def accumulate(kj, probs, accs):
        vb = vb_scr[pl.ds(pl.multiple_of(kj * tk, tk), tk), :]
        return tuple(alpha * acc + _dot(pa, vb[0:tq]) + _dot(pb, vb[tq:tk])
                     for (_, alpha, pa, pb), acc in zip(probs, accs))

    def trips(width, first, count, carry):
        def trip(i, carry):
            kjs = [first + i * width + u for u in range(width)]
            sc = [scores(kj) for kj in kjs]
            ms, accs = [c[0] for c in carry], [c[1] for c in carry]
            for kj, s in zip(kjs, sc):
                probs = softmax(s, ms)
                ms = [p[0] for p in probs]
                accs = accumulate(kj, probs, accs)
            return tuple(zip(ms, accs))

        return lax.fori_loop(0, count, trip, carry), first + count * width

    init = (jnp.full((tq, 1), NEG, F32), jnp.zeros((tq, 2 * dv), F32))
    carry, done = (init,) * len(chains), 0
    for width in DIFF_TRIP_WIDTHS:
        carry, done = trips(width, done, (n_chunks - done) // width, carry)
    lam = _lam_from(lamc_ref[...], lam_init)
    for r in range(2):
        (_, acc0), (_, acc1) = carry[2 * r], carry[2 * r + 1]
        o = acc0[:, 0:dv] / acc0[:, dv:dv + 1] - lam * (acc1[:, 0:dv] / acc1[:, dv:dv + 1])
        oc = _rms(o, subln_ref[...]) * (1.0 - lam_init)
        o_ref[0, r * tq:(r + 1) * tq, :] = oc * _silu(g_ref[0, r * tq:(r + 1) * tq, :])


def _diff_bias_tiles(rel_bias, tq, n_const, n_heads):
    dist = (np.arange(-1, n_const + 1)[:, None, None] * tq + np.arange(tq)[None, :, None]
            - np.arange(tq)[None, None, :]).reshape((n_const + 2) * tq, tq)
    tiles = _table_by_column(rel_bias, _bucket_code(np.clip(dist, 0, None), dist >= 0))
    return tiles.reshape(2, n_heads, n_const + 2, tq, tq)


def _const_tile_index(tq):
    exact = N_BUCKETS // 2
    sat = math.ceil(exact * (T5_MAX_DIST / exact) ** ((N_BUCKETS - 1 - exact) / (N_BUCKETS - exact))) + 1
    return -(-(sat + tq - 1) // tq)


def _diff_attn_prompt_call(q, k, v, g, rel_bias, subln, lamc, lam_init, tq):
    b, s, w = q.shape
    hw = 2 * DH_C
    n_heads = w // hw
    n_const = _const_tile_index(tq)
    assert s % (2 * tq) == 0
    tiles = _diff_bias_tiles(rel_bias * LOG2E, tq, n_const, n_heads)
    qspec = pl.BlockSpec((1, 2 * tq, hw), lambda i, h, j, *_: (i, j, h))
    kvspec = pl.BlockSpec((1, s, hw), lambda i, h, j, *_: (i, 0, h), pipeline_mode=pl.Buffered(1))
    in_specs = [qspec, kvspec, kvspec, qspec,
                pl.BlockSpec((2, None, n_const + 2, tq, tq), lambda i, h, j, *_: (0, h, 0, 0, 0),
                             pipeline_mode=pl.Buffered(1)),
                pl.BlockSpec((1, hw), lambda i, h, j, *_: (0, 0)),
                pl.BlockSpec((4, DH_C), lambda i, h, j, *_: (0, 0))]
    return ((b, n_heads, s // (2 * tq)),
            functools.partial(_diff_attn_body, tq=tq, n_const=n_const, lam_init=lam_init),
            (q, k, v, g, tiles, subln.reshape(1, hw), lamc), in_specs, qspec,
            jax.ShapeDtypeStruct((b, s, w), F32), [pltpu.VMEM((s, hw), BF16), pltpu.VMEM((s, 2 * hw), BF16)])


def _hgrn2_gates(q, f_logit, lb, live):
    f = lb + (1.0 - lb) * _sigmoid(f_logit)
    log_f, kk = jnp.log(f), 1.0 - f
    if live is not None:
        log_f, kk = jnp.where(live, log_f, 0.0), jnp.where(live, kk, 0.0)
    return _silu(q) * (DK_D ** -0.5), kk, log_f


def _hgrn2_body(q_ref, f_ref, v_ref, g_ref, lb_ref, gn_ref, s0_ref, y_ref, sl_ref, st_scr, o_scr,
                *, tb, chunk, n_heads):
    j = pl.program_id(1)

    @pl.when(j == 0)
    def _():
        for h in range(n_heads):
            st_scr[h] = s0_ref[0, h].T

    row = lax.broadcasted_iota(jnp.int32, (tb, DK_D), 0)
    pos = row % chunk
    r2 = lax.broadcasted_iota(jnp.int32, (tb, tb), 0)
    c2 = lax.broadcasted_iota(jnp.int32, (tb, tb), 1)
    same_chunk_causal = (r2 // chunk == c2 // chunk) & (c2 <= r2)
    heads = []
    for h in range(n_heads):
        ln = slice(h * DK_D, (h + 1) * DK_D)
        qf, kk, log_f = _hgrn2_gates(q_ref[0, :, ln], f_ref[0, :, ln], lb_ref[:, ln], None)
        v = v_ref[0, :, ln].astype(BF16)
        bc = log_f
        tot = log_f
        s = 1
        while s < chunk:
            bc = jnp.where(pos >= s, bc + pltpu.roll(bc, s, 0), bc)
            tot = tot + jnp.where((pos & s) == 0, pltpu.roll(tot, tb - s, 0), pltpu.roll(tot, s, 0))
            s *= 2
        qd = (qf * jnp.exp(bc)).astype(BF16)
        kd = (kk * jnp.exp(-bc)).astype(BF16)
        kl = (kk * jnp.exp(tot - bc)).astype(BF16)
        att = jnp.where(same_chunk_causal, _dot_nt(qd, kd), 0.0)
        o_scr[:, ln] = _dot(att.astype(BF16), v)
        heads.append((ln, qd, kl, v, jnp.exp(tot), st_scr[h]))

    for n in range(tb // chunk):
        rows = slice(n * chunk, (n + 1) * chunk)
        for h, (ln, qd, kl, v, gdec, st) in enumerate(heads):
            o_scr[rows, ln] += _dot_nt(qd[rows], st.astype(BF16))
            heads[h] = (ln, qd, kl, v, gdec, st * gdec[n * chunk:n * chunk + 1, :] + _dot_tn(v[rows], kl[rows]))

    for h, (ln, _, _, _, _, st) in enumerate(heads):
        st_scr[h] = st
        y_ref[0, :, ln] = _rms(o_scr[:, ln], gn_ref[...]) * _silu(g_ref[0, :, ln])

    @pl.when(j == pl.num_programs(1) - 1)
    def _():
        for h in range(n_heads):
            sl_ref[0, h] = st_scr[h].T


def _hgrn2_prompt(q, f, v, g, lb, gnorm, s0, tb):
    b, s, w = q.shape
    n_heads = w // DK_D
    seq = pl.BlockSpec((1, tb, w), lambda i, j: (i, j, 0))
    st = pl.BlockSpec((1, n_heads, DK_D, DV_D), lambda i, j: (i, 0, 0, 0))
    return pl.pallas_call(
        functools.partial(_hgrn2_body, tb=tb, chunk=CHUNK_D, n_heads=n_heads),
        grid=(b, s // tb),
        in_specs=[seq, seq, seq, seq,
                  pl.BlockSpec((1, w), lambda i, j: (0, 0)),
                  pl.BlockSpec((1, DV_D), lambda i, j: (0, 0)),
                  st],
        out_specs=[seq, st],
        out_shape=[jax.ShapeDtypeStruct((b, s, w), F32),
                   jax.ShapeDtypeStruct((b, n_heads, DK_D, DV_D), F32)],
        scratch_shapes=[pltpu.VMEM((n_heads, DV_D, DK_D), F32), pltpu.VMEM((tb, w), F32)],
        compiler_params=_cparams(("parallel", "arbitrary"), 40),
        name="hgrn2_prompt",
    )(q, f, v, g, lb.reshape(1, w), gnorm.reshape(1, DV_D), s0)


ROWS8 = 8


def _dil_decode_body(q_ref, kn_ref, vn_ref, gb_ref, kt_ref, vt_ref, tab_ref, tabn_ref, y_ref, *, n_heads):
    n_pat = len(DIL_PATTERNS)
    for h in range(n_heads):
        q = (q_ref[0, h] * (DH_B ** -0.5)).astype(BF16)
        s = _dot(q, kt_ref[0, h].astype(BF16))
        sn = _dot_nt(q, kn_ref[0, h].astype(BF16))
        parts = [s + tab_ref[h, g * ROWS8:(g + 1) * ROWS8] for g in range(n_pat)]
        parts += [sn + tabn_ref[h, g * ROWS8:(g + 1) * ROWS8, 0:ROWS8] for g in range(n_pat)]
        m = parts[0].max(axis=-1, keepdims=True)
        for p in parts[1:]:
            m = jnp.maximum(m, p.max(axis=-1, keepdims=True))
        ps = [jnp.exp(p - m) for p in parts]
        l = ps[0].sum(axis=-1, keepdims=True)
        for p in ps[1:]:
            l = l + p.sum(axis=-1, keepdims=True)
        pw = (ps[0] + ps[1] + ps[2]).astype(BF16)
        pn = (ps[3] + ps[4] + ps[5]).astype(BF16)
        o = _dot_nt(pw, vt_ref[0, h].astype(BF16)) + _dot(pn, vn_ref[0, h].astype(BF16))
        y_ref[0, h] = (o / l) * _silu(gb_ref[0, h])


def _dil_decode(q, k, v, gb, win_k, win_v, rel_bias):
    b, nt, n_heads, dh = q.shape
    l_buf = win_k.shape[1]
    assert nt <= ROWS8 and rel_bias.shape[1] == n_heads
    tt = np.tile(np.arange(ROWS8), len(DIL_PATTERNS))[:, None]
    dil = np.repeat([d for _, d in DIL_PATTERNS], ROWS8)[:, None]
    live = tt < nt

    def code(dd, lo):
        return _bucket_code(np.clip(dd, 0, None), live & (dd % dil == 0) & (dd >= lo * dil) & (dd <= BAND * dil))

    tab = _table_by_column(rel_bias, code(l_buf + tt - np.arange(l_buf)[None, :], 1))
    tn = np.arange(128)[None, :]
    tabn = _table_by_column(rel_bias, np.where(tn < nt, code(tt - tn, 0), -1).astype(np.int32))

    by_head = lambda z: jnp.pad(jnp.transpose(z, (0, 2, 1, 3)), ((0, 0), (0, 0), (0, ROWS8 - nt), (0, 0)))
    pos_minor = lambda z: jnp.transpose(z, (0, 2, 3, 1))
    new = pl.BlockSpec((1, n_heads, ROWS8, dh), lambda i: (i, 0, 0, 0))
    win = pl.BlockSpec((1, n_heads, dh, l_buf), lambda i: (i, 0, 0, 0))
    const = lambda a: pl.BlockSpec(a.shape, lambda i: (0,) * a.ndim)
    y = pl.pallas_call(
        functools.partial(_dil_decode_body, n_heads=n_heads),
        grid=(b,),
        in_specs=[new, new, new, new, win, win, const(tab), const(tabn)],
        out_specs=new,
        out_shape=jax.ShapeDtypeStruct((b, n_heads, ROWS8, dh), F32),
        compiler_params=_cparams(("parallel",), 40),
        name="dil_decode",
    )(by_head(q), by_head(k), by_head(v), by_head(gb), pos_minor(win_k), pos_minor(win_v), tab, tabn)
    return jnp.transpose(y[:, :, :nt], (0, 2, 1, 3))


def _diff_decode_body(pt_ref, qm_ref, kn_ref, vn_ref, g_ref, tab_ref, tabn_ref, subln_ref, lamc_ref, *rest,
                      nt, n_heads, n_pages, lam_init):
    del pt_ref
    k_pages, v_pages, y_ref = rest[:n_pages], rest[n_pages:2 * n_pages], rest[2 * n_pages]
    qm = (qm_ref[0] * (DH_C ** -0.5)).astype(BF16)
    rows, hw = qm.shape
    grp = 2 * n_heads
    pw = PAGE * n_heads
    ss = [_dot_nt(qm, k_pages[p][0].astype(BF16)) + tab_ref[:, p * pw:(p + 1) * pw] for p in range(n_pages)]
    sn = _dot_nt(qm, kn_ref[0].astype(BF16)) + tabn_ref[:, 0:kn_ref.shape[1]]
    m = sn.max(axis=-1, keepdims=True)
    for s in ss:
        m = jnp.maximum(m, s.max(axis=-1, keepdims=True))
    ps = [jnp.exp(s - m) for s in ss]
    pn = jnp.exp(sn - m)
    l = pn.sum(axis=-1, keepdims=True)
    for p in ps:
        l = l + p.sum(axis=-1, keepdims=True)
    lam = _lam_from(lamc_ref[...], lam_init)
    row_map = (lax.broadcasted_iota(jnp.int32, (rows, 1), 0) % grp) // n_heads
    coef = jnp.where(row_map == 0, 1.0, -lam) / l
    o = _dot((pn * coef).astype(BF16), vn_ref[0].astype(BF16))
    for p in range(n_pages):
        o = o + _dot((ps[p] * coef).astype(BF16), v_pages[p][0].astype(BF16))
    o = o.reshape(nt, grp, hw)
    oc = o[:, 0:n_heads, :] + o[:, n_heads:grp, :]
    y_ref[0] = _rms(oc, subln_ref[...]) * (1.0 - lam_init) * _silu(g_ref[0])


def _diff_decode_call(q, k, v, g, cache_k, cache_v, page_table, rel_bias, subln, lamc, lam_init):
    b, nt, w = q.shape
    hw = 2 * DH_C
    n_heads = w // hw
    n_phys = cache_k.shape[0]
    n_pages = page_table.shape[1]
    past = n_pages * PAGE
    grp = 2 * n_heads
    rows = nt * grp
    assert rel_bias.shape[1] == grp
    tt = np.arange(rows)[:, None] // grp
    hq = np.arange(rows)[:, None] % n_heads
    pos = lambda n: np.arange(n)[None, :] // n_heads
    same = lambda n: np.arange(n)[None, :] % n_heads == hq
    tab = _table_by_row(rel_bias, _bucket_code(past + tt - pos(past * n_heads), same(past * n_heads)))
    jn = tt - pos(128)
    tabn = _table_by_row(rel_bias, _bucket_code(np.clip(jn, 0, None), same(128) & (jn >= 0) & (pos(128) < nt)))

    map_mask = (np.arange(hw)[None, :] // DH_C == np.arange(2)[:, None]).astype(np.float32)
    qm = q.reshape(b, nt, 1, n_heads, hw) * map_mask[None, None, :, None, :]
    by_head = lambda z: z.reshape(b, nt * n_heads, hw)

    per_b = lambda *shape: pl.BlockSpec((1,) + shape, lambda i, pt: (i,) + (0,) * len(shape))
    const = lambda *shape: pl.BlockSpec(shape, lambda i, pt: (0,) * len(shape))
    page = lambda p: pl.BlockSpec((1, PAGE * n_heads, hw), lambda i, pt, p=p: (pt[i * n_pages + p], 0, 0))
    in_specs = ([per_b(rows, hw), per_b(nt * n_heads, hw), per_b(nt * n_heads, hw), per_b(nt, n_heads, hw),
                 const(rows, past * n_heads), const(rows, 128), const(1, hw), const(4, DH_C)]
                + [page(p) for p in range(n_pages)] * 2)
    rows_of = lambda c: c.reshape(n_phys, PAGE * n_heads, hw)
    args = (qm.reshape(b, rows, hw), by_head(k), by_head(v), g.reshape(b, nt, n_heads, hw),
            tab, tabn, subln.reshape(1, hw), lamc, *([rows_of(cache_k)] * n_pages), *([rows_of(cache_v)] * n_pages))
    return (functools.partial(_diff_decode_body, nt=nt, n_heads=n_heads, n_pages=n_pages, lam_init=lam_init),
            args, in_specs, per_b(nt, n_heads, hw), jax.ShapeDtypeStruct((b, nt, n_heads, hw), F32))


def _diff_attn_both_body(pt_ref, *refs, prompt_body, decode_body, n_prompt_in, n_decode_in):
    n_in = n_prompt_in + n_decode_in
    o_ref, y_ref = refs[n_in], refs[n_in + 1]
    prompt_body(*refs[:n_prompt_in], o_ref, *refs[n_in + 2:])
    decode_body(pt_ref, *refs[n_prompt_in:n_in], y_ref)


def _diff_attn(prompt, decode, page_table, common):
    grid, p_body, p_args, p_specs, p_out, p_shape, scratch = _diff_attn_prompt_call(*prompt, *common)
    d_body, d_args, d_specs, d_out, d_shape = _diff_decode_call(*decode, page_table, *common[:-1])
    assert d_shape.shape[0] == math.prod(grid), "one decode sequence per prompt grid step"

    def at_step(spec):
        seq = lambda i, h, j, pt: (i * grid[1] + h) * grid[2] + j
        return pl.BlockSpec(spec.block_shape, lambda i, h, j, pt, f=spec.index_map: f(seq(i, h, j, pt), pt))

    grid_spec = pltpu.PrefetchScalarGridSpec(
        num_scalar_prefetch=1, grid=grid,
        in_specs=list(p_specs) + [at_step(sp) for sp in d_specs],
        out_specs=[p_out, at_step(d_out)],
        scratch_shapes=scratch)
    return pl.pallas_call(
        functools.partial(_diff_attn_both_body, prompt_body=p_body, decode_body=d_body,
                          n_prompt_in=len(p_specs), n_decode_in=len(d_specs)),
        grid_spec=grid_spec,
        out_shape=[p_shape, d_shape],
        compiler_params=_cparams(("arbitrary", "arbitrary", "arbitrary"), 60),
        name="diff_attn",
    )(page_table.reshape(-1), *p_args, *d_args)


def _hgrn2_decode_body(q_ref, f_ref, ft_ref, v_ref, g_ref, lb_ref, lbc_ref, gn_ref, s0_ref, y_ref, sl_ref,
                       *, nt, n_heads):
    row = lax.broadcasted_iota(jnp.int32, (ROWS8, DK_D), 0)
    r2 = lax.broadcasted_iota(jnp.int32, (ROWS8, ROWS8), 0)
    c2 = lax.broadcasted_iota(jnp.int32, (ROWS8, ROWS8), 1)
    for h in range(n_heads):
        ln = slice(h * DK_D, (h + 1) * DK_D)
        qf, kk, log_f = _hgrn2_gates(q_ref[0, :, ln], f_ref[0, :, ln], lb_ref[:, ln], row < nt)
        v = v_ref[0, :, ln].astype(BF16)
        bc = log_f
        s = 1
        while s < ROWS8:
            bc = jnp.where(row >= s, bc + pltpu.roll(bc, s, 0), bc)
            s *= 2
        tot = bc[ROWS8 - 1:ROWS8, :]
        qd = (qf * jnp.exp(bc)).astype(BF16)
        kd = (kk * jnp.exp(-bc)).astype(BF16)
        kl = (kk * jnp.exp(tot - bc)).astype(BF16)
        s0 = s0_ref[0, h]
        att = jnp.where(c2 <= r2, _dot_nt(qd, kd), 0.0)
        o = _dot(att.astype(BF16), v) + _dot(qd, s0.astype(BF16))
        lbc = lbc_ref[h]
        fc = lbc + (1.0 - lbc) * _sigmoid(ft_ref[0, h])
        decay = fc[:, 0:1]
        for t in range(1, nt):
            decay = decay * fc[:, t:t + 1]
        sl_ref[0, h] = decay * s0 + _dot_tn(kl, v)
        y_ref[0, :, ln] = _rms(o, gn_ref[...]) * _silu(g_ref[0, :, ln])


def _hgrn2_decode(q, f, v, g, lb, gnorm, s0):
    b, nt, w = q.shape
    n_heads = w // DK_D
    assert nt <= ROWS8
    pad = lambda z: jnp.pad(z, ((0, 0), (0, ROWS8 - nt), (0, 0)))
    cols = jnp.transpose(f.reshape(b, nt, n_heads, DK_D), (0, 2, 3, 1))
    per_b = lambda *shape: pl.BlockSpec((1,) + shape, lambda i: (i,) + (0,) * len(shape))
    const = lambda *shape: pl.BlockSpec(shape, lambda i: (0,) * len(shape))
    y, s_last = pl.pallas_call(
        functools.partial(_hgrn2_decode_body, nt=nt, n_heads=n_heads),
        grid=(b,),
        in_specs=[per_b(ROWS8, w), per_b(ROWS8, w), per_b(n_heads, DK_D, nt), per_b(ROWS8, w), per_b(ROWS8, w),
                  const(1, w), const(n_heads, DK_D, 1), const(1, DV_D), per_b(n_heads, DK_D, DV_D)],
        out_specs=[per_b(ROWS8, w), per_b(n_heads, DK_D, DV_D)],
        out_shape=[jax.ShapeDtypeStruct((b, ROWS8, w), F32),
                   jax.ShapeDtypeStruct((b, n_heads, DK_D, DV_D), F32)],
        compiler_params=_cparams(("parallel",), 32),
        name="hgrn2_decode",
    )(pad(q), pad(f), cols, pad(v), pad(g), lb.reshape(1, w), lb.reshape(n_heads, DK_D, 1), gnorm.reshape(1, DV_D),
      s0)
    return y[:, :nt], s_last


def _layer_ab(x, group, w_in, gate_params, w_out, rel_bias, state):
    b, t, d = x.shape
    c = w_out.shape[0] // 2
    xa, ga, q, k, v, gb = [z.reshape(b, t, c) for z in _norm_proj(x.reshape(b * t, d), *group, w_in, (c,) * 6,
                                                                    min(512, b * t))]
    by_head = lambda z: z.reshape(z.shape[0], z.shape[1], c // DH_B, DH_B)
    if state is None:
        ya, conv_new, h_last = _rglru_prompt(xa, ga, jnp.zeros((b, CONV_W - 1, c), F32), jnp.zeros((b, c), F32),
                                             gate_params, tb=256)
        h_last = h_last.reshape(b, c)
        yb = _dil_prompt(q, k, v, gb, rel_bias)
        keep = min(WIN_MAX, t)
        k_rows, v_rows = by_head(k[:, t - keep:]), by_head(v[:, t - keep:])
    else:
        conv_buf, h0, win_k, win_v = state
        ya, conv_new, h_last = _rglru_decode(xa, ga, conv_buf, h0, gate_params)
        k_rows, v_rows = by_head(k), by_head(v)
        yb = _dil_decode(by_head(q), k_rows, v_rows, by_head(gb), win_k, win_v, rel_bias)
    x = _out_proj(x.reshape(b * t, d), ya.reshape(b * t, c), yb.reshape(b * t, c), w_out, None, min(512, b * t))
    return x.reshape(b, t, d), conv_new, h_last, k_rows, v_rows


def _layer_cd(xp, xs, group, w_in, lamc, subln, lb, gnorm, w_out, rel_bias, lam_init, final_g, state):
    c = w_out.shape[0] // 2
    heads = c // (2 * DH_C)

    def project(x):
        b, t, d = x.shape
        return [z.reshape(b, t, c) for z in _norm_proj(x.reshape(b * t, d), *group, w_in, (c,) * 8, min(512, b * t))]

    def finish(x, yc, yd, kc, vc, s_last):
        b, t, d = x.shape
        y = _out_proj(x.reshape(b * t, d), yc.reshape(b * t, c), yd.reshape(b * t, c), w_out, final_g,
                      min(512, b * t))
        return (y.reshape(b, t, d), kc.reshape(b, t, heads, 2 * DH_C), vc.reshape(b, t, heads, 2 * DH_C), s_last)

    qc, kc, vc, gc, qd, fd, idd, gd = project(xp)
    qc2, kc2, vc2, gc2, qd2, fd2, idd2, gd2 = project(xs)
    s0, cache_k, cache_v, page_table = state
    yc, yc2 = _diff_attn((qc, kc, vc, gc), (qc2, kc2, vc2, gc2, cache_k, cache_v), page_table,
                         (rel_bias, subln, lamc, lam_init, 256))
    yd, s_last = _hgrn2_prompt(qd, fd, idd, gd, lb, gnorm, jnp.zeros((xp.shape[0], c // DK_D, DK_D, DV_D), F32),
                               tb=256)
    yd2, s_last2 = _hgrn2_decode(qd2, fd2, idd2, gd2, lb, gnorm, s0)
    return finish(xp, yc, yd, kc, vc, s_last), finish(xs, yc2, yd2, kc2, vc2, s_last2)


def kernel(x_prompt, x_sample, state_conv_a, state_h_a, cache_win_k, cache_win_v, cache_k_c, cache_v_c, state_s_d,
           page_table, norm_g, norm_final, rel_bias, w_in_ab, conv_w_a, conv_b_a, w_r_a, b_r_a, w_i_a, b_i_a, lam_a,
           w_out_ab, w_in_cd, lam_c, subln_c, lb_d, gnorm_d, w_out_cd):
    depth = norm_g.shape[0]
    assert depth == 2
    lb_soft = jax.nn.softmax(lb_d.astype(F32), axis=0)
    lb_all = jnp.cumsum(lb_soft, axis=0) - lb_soft[0]

    gate_params = _gate_params(conv_w_a[0], conv_b_a[0], w_r_a[0], b_r_a[0], w_i_a[0], b_i_a[0], lam_a[0])
    xp, conv_p, h_p, wk_p, wv_p = _layer_ab(x_prompt, (norm_g[0],), w_in_ab[0], gate_params, w_out_ab[0], rel_bias,
                                            None)
    xs, conv_s, h_s, wk_s, wv_s = _layer_ab(x_sample, (norm_g[0],), w_in_ab[0], gate_params, w_out_ab[0], rel_bias,
                                            (state_conv_a[0], state_h_a[0], cache_win_k[0], cache_win_v[0]))
    lam_init = 0.8 - 0.6 * math.exp(-0.3 * 1)
    cd = (w_in_cd[0], lam_c[0], subln_c[0], lb_all[1], gnorm_d[0], w_out_cd[0], rel_bias, lam_init, norm_final)
    (yp, kc_p, vc_p, s_p), (ys, kc_s, vc_s, s_s) = _layer_cd(
        xp, xs, (norm_g[1],), *cd, (state_s_d[0], cache_k_c[0], cache_v_c[0], page_table))
    e = lambda z: z[None]
    return (yp, ys, e(conv_p), e(h_p), e(wk_p), e(wv_p), e(kc_p), e(vc_p), e(s_p),
            e(conv_s), e(h_s), e(wk_s), e(wv_s), e(kc_s), e(vc_s), e(s_s))
```

```python
import functools
import math

import numpy as np
import jax
import jax.numpy as jnp
from jax import lax
from jax.experimental import pallas as pl
from jax.experimental.pallas import tpu as pltpu

F32 = jnp.float32
BF16 = jnp.bfloat16

N_BLK_A = 8
CONV_W = 4
RG_C = 8.0
DH_B = 64
DIL_PATTERNS = ((128, 1), (512, 4), (2048, 16))
BAND = 128
WIN_MAX = 2048
DH_C = 64
DK_D = 128
DV_D = 128
CHUNK_D = 16
N_BUCKETS = 32
T5_MAX_DIST = 2048
PAGE = 128
EPS = 1e-6
NEG = -1e30
MIB = 2 ** 20


def _cparams(sem, vmem_mib):
    return pltpu.CompilerParams(dimension_semantics=sem, vmem_limit_bytes=vmem_mib * MIB)


def _dot(a, b):
    return jnp.dot(a, b, preferred_element_type=F32)


def _dot_nt(a, b):
    return lax.dot_general(a, b, (((1,), (1,)), ((), ())), preferred_element_type=F32)


def _dot_tn(a, b):
    return lax.dot_general(a, b, (((0,), (0,)), ((), ())), preferred_element_type=F32)


def _sigmoid(x):
    return 1.0 / (1.0 + jnp.exp(-x))


def _silu(x):
    return x * _sigmoid(x)


def _rms(x, g):
    return x * lax.rsqrt(jnp.mean(x * x, axis=-1, keepdims=True) + EPS) * g


def _t5_bucket(dist):
    n = np.maximum(np.asarray(dist), 0)
    exact = N_BUCKETS // 2
    nf = np.maximum(n, 1).astype(np.float32)
    val = np.log(nf / np.float32(exact)) / np.float32(math.log(T5_MAX_DIST / exact)) * np.float32(N_BUCKETS - exact)
    large = np.minimum(exact + val.astype(np.int32), N_BUCKETS - 1)
    return np.where(n < exact, n, large).astype(np.int32)


def _norm_proj_body(x_ref, g_ref, w_ref, *o_refs):
    hn = _rms(x_ref[...], g_ref[...]).astype(BF16)
    off = 0
    for o_ref in o_refs:
        w = o_ref.shape[-1]
        o_ref[...] = _dot(hn, w_ref[:, off:off + w])
        off += w


def _norm_proj(x, g, w, widths, tm):
    t, d = x.shape
    n = w.shape[1]
    return pl.pallas_call(
        _norm_proj_body,
        grid=(t // tm,),
        in_specs=[pl.BlockSpec((tm, d), lambda i: (i, 0)),
                  pl.BlockSpec((1, d), lambda i: (0, 0)),
                  pl.BlockSpec((d, n), lambda i: (0, 0))],
        out_specs=[pl.BlockSpec((tm, wd), lambda i: (i, 0)) for wd in widths],
        out_shape=[jax.ShapeDtypeStruct((t, wd), F32) for wd in widths],
        compiler_params=_cparams(("parallel",), 56),
        name="norm_proj",
    )(x, g.reshape(1, d), w.astype(BF16))


def _rglru_gates(xc, wr, br, wi, bi, lam):
    xcb = xc.astype(BF16)
    r = _sigmoid(_dot(xcb, wr) + br)
    gi = _sigmoid(_dot(xcb, wi) + bi)
    nl = -lam
    softplus = jnp.maximum(nl, 0.0) + jnp.log(1.0 + jnp.exp(-jnp.abs(nl)))
    log_a = (-RG_C) * r * softplus
    a = jnp.exp(log_a)
    u = jnp.sqrt(1.0 - jnp.exp(2.0 * log_a)) * (gi * xc)
    return a, u


def _rglru_prompt_body(xa_ref, ga_ref, cb_ref, h0_ref, cw_ref, cbias_ref, wr_ref, br_ref, wi_ref, bi_ref,
                       lam_ref, ya_ref, cnew_ref, hl_ref, xp_scr, h_scr, *, tb):
    j = pl.program_id(1)

    @pl.when(j == 0)
    def _():
        xp_scr[5:8, :] = cb_ref[0]
        h_scr[...] = h0_ref[0]

    xp_scr[8:8 + tb, :] = xa_ref[0]
    cw = cw_ref[...]
    xc = cbias_ref[...] + xp_scr[5:5 + tb, :] * cw[0:1, :]
    for k in range(1, CONV_W):
        xc = xc + xp_scr[5 + k:5 + k + tb, :] * cw[k:k + 1, :]
    tail = xp_scr[5 + tb:8 + tb, :]
    cnew_ref[0] = tail
    xp_scr[5:8, :] = tail

    a, u = _rglru_gates(xc, wr_ref[...], br_ref[...], wi_ref[...], bi_ref[...], lam_ref[...])
    row = lax.broadcasted_iota(jnp.int32, a.shape, 0)
    s = 1
    while s < tb:
        keep = row >= s
        u = jnp.where(keep, a * pltpu.roll(u, s, 0) + u, u)
        a = jnp.where(keep, a * pltpu.roll(a, s, 0), a)
        s *= 2
    h = a * h_scr[...] + u
    h_last = h[tb - 1:tb, :]
    h_scr[...] = h_last
    hl_ref[0] = h_last
    ya_ref[0] = h * _silu(ga_ref[0])


def _rglru_decode_body(xa_ref, ga_ref, cb_ref, h0_ref, cw_ref, cbias_ref, wr_ref, br_ref, wi_ref, bi_ref,
                       lam_ref, ya_ref, cnew_ref, hl_ref):
    nt = xa_ref.shape[0]
    xp = [cb_ref[i] for i in range(CONV_W - 1)] + [xa_ref[t] for t in range(nt)]
    cw = cw_ref[...]
    h = h0_ref[...]
    for t in range(nt):
        xc = cbias_ref[...] + xp[t] * cw[0:1, :]
        for k in range(1, CONV_W):
            xc = xc + xp[t + k] * cw[k:k + 1, :]
        a, u = _rglru_gates(xc, wr_ref[...], br_ref[...], wi_ref[...], bi_ref[...], lam_ref[...])
        h = a * h + u
        ya_ref[t] = h * _silu(ga_ref[t])
    for i in range(CONV_W - 1):
        cnew_ref[i] = xp[nt + i]
    hl_ref[...] = h


def _block_diag(w):
    g, bi, bj = w.shape
    eye = jnp.eye(g, dtype=w.dtype)
    return (w[:, :, None, :] * eye[:, None, :, None]).reshape(g * bi, g * bj)


def _gate_params(conv_w, conv_b, w_r, b_r, w_i, b_i, lam):
    c = conv_w.shape[-1]
    return (conv_w, conv_b.reshape(1, c), _block_diag(w_r).astype(BF16), b_r.reshape(1, c),
            _block_diag(w_i).astype(BF16), b_i.reshape(1, c), lam.reshape(1, c))


def _rglru_prompt_call(xa, ga, conv_buf, h0, gate_params, tb):
    b, t, c = xa.shape
    full = lambda shape: pl.BlockSpec(shape, lambda i, j: (0,) * len(shape))
    seq = pl.BlockSpec((1, tb, c), lambda i, j: (i, j, 0))
    in_specs = [seq, seq,
                pl.BlockSpec((1, CONV_W - 1, c), lambda i, j: (i, 0, 0)),
                pl.BlockSpec((1, 1, c), lambda i, j: (i, 0, 0)),
                full((CONV_W, c)), full((1, c)), full((c, c)), full((1, c)), full((c, c)), full((1, c)),
                full((1, c))]
    out_specs = [seq,
                 pl.BlockSpec((1, CONV_W - 1, c), lambda i, j: (i, 0, 0)),
                 pl.BlockSpec((1, 1, c), lambda i, j: (i, 0, 0))]
    out_shapes = [jax.ShapeDtypeStruct((b, t, c), F32),
                  jax.ShapeDtypeStruct((b, CONV_W - 1, c), F32),
                  jax.ShapeDtypeStruct((b, 1, c), F32)]
    return ((b, t // tb), functools.partial(_rglru_prompt_body, tb=tb),
            (xa, ga, conv_buf, h0.reshape(b, 1, c), *gate_params), in_specs, out_specs, out_shapes,
            [pltpu.VMEM((tb + 8, c), F32), pltpu.VMEM((1, c), F32)])


def _rglru_decode(xa, ga, conv_buf, h0, gate_params):
    b, t, c = xa.shape
    tm = lambda z: jnp.transpose(z, (1, 0, 2))
    ya, cnew, hl = pl.pallas_call(
        _rglru_decode_body,
        out_shape=[jax.ShapeDtypeStruct((t, b, c), F32),
                   jax.ShapeDtypeStruct((CONV_W - 1, b, c), F32),
                   jax.ShapeDtypeStruct((b, c), F32)],
        compiler_params=pltpu.CompilerParams(vmem_limit_bytes=40 * MIB),
        name="rglru_decode",
    )(tm(xa), tm(ga), tm(conv_buf), h0, *gate_params)
    return tm(ya), tm(cnew), hl


TABLE_VREGS = 16


def _lane_chunk(x):
    return next(c for c in (2048, 1024, 512, 256, 128) if x % c == 0)


def _table_by_column_body(vals_ref, code_ref, out_ref, *, rc):
    base = pl.program_id(0) * N_BUCKETS

    def chunk(ci, carry):
        r0 = pl.multiple_of(ci * rc, rc)
        code = code_ref[pl.ds(r0, rc), :]
        look = lambda k, out: jnp.where(code == k, vals_ref[base + k], out)
        out_ref[0, pl.ds(r0, rc), :] = lax.fori_loop(0, N_BUCKETS, look, jnp.full(code.shape, NEG, F32))
        return carry

    lax.fori_loop(0, code_ref.shape[0] // rc, chunk, 0)


def _table_by_column(rel_bias, code):
    r, x = code.shape
    n_col = rel_bias.shape[1]
    rc = min(r, max(8, TABLE_VREGS * 1024 // x))
    assert r % rc == 0
    return pl.pallas_call(
        functools.partial(_table_by_column_body, rc=rc),
        grid=(n_col,),
        in_specs=[pl.BlockSpec(memory_space=pltpu.SMEM), pl.BlockSpec((r, x), lambda g: (0, 0))],
        out_specs=pl.BlockSpec((1, r, x), lambda g: (g, 0, 0)),
        out_shape=jax.ShapeDtypeStruct((n_col, r, x), F32),
        compiler_params=_cparams(("parallel",), 32),
        name="bias_table_by_column",
    )(rel_bias.T.reshape(-1), code)


def _table_by_row_body(rbt_ref, code_ref, out_ref, *, lc):
    rbt = rbt_ref[...]
    n_col = rbt.shape[0]

    def group(gi, carry):
        r0 = pl.multiple_of(gi * n_col, n_col)
        for x0 in range(0, code_ref.shape[1], lc):
            code = code_ref[pl.ds(r0, n_col), x0:x0 + lc]
            out = jnp.full(code.shape, NEG, F32)
            for k in range(N_BUCKETS):
                out = jnp.where(code == k, rbt[:, k:k + 1], out)
            out_ref[pl.ds(r0, n_col), x0:x0 + lc] = out
        return carry

    lax.fori_loop(0, code_ref.shape[0] // n_col, group, 0)


def _table_by_row(rel_bias, code):
    r, x = code.shape
    return pl.pallas_call(
        functools.partial(_table_by_row_body, lc=_lane_chunk(x)),
        out_shape=jax.ShapeDtypeStruct((r, x), F32),
        name="bias_table_by_row",
    )(rel_bias.T, code)


def _bucket_code(dist, valid):
    return np.where(valid, _t5_bucket(dist), -1).astype(np.int32)


DIL_GROUP = 4


def _rows(start, dil):
    return pl.ds(start, BAND, stride=dil) if dil > 1 else pl.ds(start, BAND)


def _dil_prompt_body(q_ref, k_ref, v_ref, gb_ref, bt_ref, y_ref, k_scr, v_scr, o_scr, lse_scr, *, unit, n_heads):
    n = pl.program_id(2)

    @pl.when(n == 0)
    def _():
        k_scr[0:unit, :] = jnp.zeros((unit, k_scr.shape[1]), F32)
        v_scr[0:unit, :] = jnp.zeros((unit, v_scr.shape[1]), F32)

    k_scr[unit:2 * unit, :] = k_ref[0]
    v_scr[unit:2 * unit, :] = v_ref[0]
    for g, (_, dil) in enumerate(DIL_PATTERNS):
        span = BAND * dil

        def blocks(ci, carry, g=g, dil=dil, span=span):
            lane_head = lax.broadcasted_iota(jnp.int32, (BAND, k_scr.shape[1]), 1) // DH_B
            ones = jnp.ones((BAND, k_scr.shape[1]), BF16)
            loaded = []
            for u in range(DIL_GROUP):
                c = ci * DIL_GROUP + u
                sub = c // dil
                base = sub * span + c % dil
                if dil == 1:
                    base = pl.multiple_of(base, BAND)
                loaded.append((base, jnp.logical_or(n > 0, sub > 0),
                               q_ref[0, _rows(base, dil), :] * (DH_B ** -0.5),
                               k_scr[_rows(unit + base, dil), :].astype(BF16),
                               k_scr[_rows(unit + base - span, dil), :].astype(BF16),
                               v_scr[_rows(unit + base, dil), :].astype(BF16),
                               v_scr[_rows(unit + base - span, dil), :].astype(BF16)))
            scores = []
            for _, has_prev, q, kc, kp, _, _ in loaded:
                for h in range(n_heads):
                    qh = jnp.where(lane_head == h, q, 0.0).astype(BF16)
                    scores.append((jnp.where(has_prev, _dot_nt(qh, kp) + bt_ref[h, g, :, 0:BAND], NEG),
                                   _dot_nt(qh, kc) + bt_ref[h, g, :, BAND:2 * BAND]))
            probs = []
            for sp, sc in scores:
                m = jnp.max(jnp.maximum(sp, sc), axis=-1, keepdims=True)
                probs.append((m, jnp.exp(sp - m).astype(BF16), jnp.exp(sc - m).astype(BF16)))
            for u, (base, _, _, _, _, vc, vp) in enumerate(loaded):
                o_sel = lse_sel = None
                for h in range(n_heads):
                    m, pp, pc = probs[u * n_heads + h]
                    l = _dot(pp, ones) + _dot(pc, ones)
                    o = (_dot(pp, vp) + _dot(pc, vc)) / l
                    lse = m + jnp.log(l)
                    o_sel = o if h == 0 else jnp.where(lane_head == h, o, o_sel)
                    lse_sel = lse if h == 0 else jnp.where(lane_head == h, lse, lse_sel)
                o_scr[g, _rows(base, dil), :] = o_sel
                lse_scr[g, _rows(base, dil), :] = lse_sel
            return carry

        lax.fori_loop(0, unit // BAND // DIL_GROUP, blocks, 0)

    lse1, lse2, lse3 = lse_scr[0], lse_scr[1], lse_scr[2]
    m = jnp.maximum(jnp.maximum(lse1, lse2), lse3)
    e1, e2, e3 = jnp.exp(lse1 - m), jnp.exp(lse2 - m), jnp.exp(lse3 - m)
    ob = (e1 * o_scr[0] + e2 * o_scr[1] + e3 * o_scr[2]) / (e1 + e2 + e3)
    y_ref[0] = ob * _silu(gb_ref[0])
    k_scr[0:unit, :] = k_ref[0]
    v_scr[0:unit, :] = v_ref[0]


def _dil_prompt(q, k, v, gb, rel_bias):
    b, s, w = q.shape
    unit = BAND * DIL_PATTERNS[-1][1]
    lw = 2 * DH_B
    n_heads = w // DH_B
    idx = np.arange(BAND)[:, None] + BAND - np.arange(2 * BAND)[None, :]
    ok = (idx >= 0) & (idx <= BAND)
    code = np.concatenate([_bucket_code(dil * np.clip(idx, 0, BAND), ok) for _, dil in DIL_PATTERNS])
    tiles = _table_by_column(rel_bias, code).reshape(n_heads, len(DIL_PATTERNS), BAND, 2 * BAND)
    seq = pl.BlockSpec((1, unit, lw), lambda i, hp, n: (i, n, hp))
    return pl.pallas_call(
        functools.partial(_dil_prompt_body, unit=unit, n_heads=lw // DH_B),
        grid=(b, w // lw, s // unit),
        in_specs=[seq, seq, seq, seq,
                  pl.BlockSpec((lw // DH_B, len(DIL_PATTERNS), BAND, 2 * BAND), lambda i, hp, n: (hp, 0, 0, 0))],
        out_specs=seq,
        out_shape=jax.ShapeDtypeStruct((b, s, w), F32),
        scratch_shapes=[pltpu.VMEM((2 * unit, lw), F32), pltpu.VMEM((2 * unit, lw), F32),
                        pltpu.VMEM((len(DIL_PATTERNS), unit, lw), F32),
                        pltpu.VMEM((len(DIL_PATTERNS), unit, lw), F32)],
        compiler_params=_cparams(("parallel", "parallel", "arbitrary"), 40),
        name="dil_prompt",
    )(q, k, v, gb, tiles)


def _out_proj_body(x_ref, ya_ref, yb_ref, wa_ref, wb_ref, *rest, final_norm):
    y = x_ref[...] + _dot(ya_ref[...].astype(BF16), wa_ref[...]) + _dot(yb_ref[...].astype(BF16), wb_ref[...])
    if final_norm:
        y = _rms(y, rest[0][...])
    rest[-1][...] = y


def _out_proj(x, ya, yb, w_out, final_g, tm):
    t, d = x.shape
    ca, cb = ya.shape[1], yb.shape[1]
    row = lambda c: pl.BlockSpec((tm, c), lambda i: (i, 0))
    const = lambda shape: pl.BlockSpec(shape, lambda i: (0, 0))
    wa = w_out[:ca].astype(BF16)
    wb = w_out[ca:].astype(BF16)
    args = [x, ya, yb, wa, wb]
    in_specs = [row(d), row(ca), row(cb), const((ca, d)), const((cb, d))]
    if final_g is not None:
        args.append(final_g.reshape(1, d))
        in_specs.append(const((1, d)))
    return pl.pallas_call(
        functools.partial(_out_proj_body, final_norm=final_g is not None),
        grid=(t // tm,),
        in_specs=in_specs,
        out_specs=row(d),
        out_shape=jax.ShapeDtypeStruct((t, d), F32),
        compiler_params=_cparams(("parallel",), 48),
        name="out_proj",
    )(*args)


def _lam_from(lamc, lam_init):
    s01 = jnp.sum(lamc[0:1, :] * lamc[1:2, :], axis=-1, keepdims=True)
    s23 = jnp.sum(lamc[2:3, :] * lamc[3:4, :], axis=-1, keepdims=True)
    return jnp.exp(s01) - jnp.exp(s23) + lam_init


LOG2E = math.log2(math.e)
DIFF_TRIP_WIDTHS = (4, 2, 1)


def _diff_attn_body(q_ref, k_ref, v_ref, g_ref, bt_ref, subln_ref, lamc_ref, o_ref, kb_scr, vb_scr,
                    *, tq, n_const, lam_init):
    qi = pl.program_id(2)
    dv = v_ref.shape[-1]
    tk = 2 * tq

    @pl.when(qi == 0)
    def _():
        kb_scr[...] = k_ref[0].astype(BF16)
        vb_scr[:, 0:dv] = v_ref[0].astype(BF16)
        lane = lax.broadcasted_iota(jnp.int32, (vb_scr.shape[0], dv), 1)
        vb_scr[:, dv:2 * dv] = jnp.where(lane == 0, 1.0, 0.0).astype(BF16)

    q = q_ref[0] * (DH_C ** -0.5 * LOG2E)
    lane = lax.broadcasted_iota(jnp.int32, q.shape, 1)
    qm = (jnp.where(lane < DH_C, q, 0.0).astype(BF16), jnp.where(lane >= DH_C, q, 0.0).astype(BF16))
    chains = [(r, c) for r in range(2) for c in range(2)]
    n_chunks = qi + 1

    def scores(kj):
        kb = kb_scr[pl.ds(pl.multiple_of(kj * tk, tk), tk), :]
        out = []
        for r, c in chains:
            d = 2 * (qi - kj) + r
            ia, ib = jnp.minimum(d, n_const) + 1, jnp.minimum(d - 1, n_const) + 1
            s = _dot_nt(qm[c][r * tq:(r + 1) * tq], kb)
            out.append((s[:, 0:tq] + bt_ref[c, ia], s[:, tq:tk] + bt_ref[c, ib]))
        return tuple(out)

    def softmax(sc, ms):
        out = []
        for (sa, sb), m in zip(sc, ms):
            m_new = jnp.maximum(m, jnp.max(jnp.maximum(sa, sb), axis=-1, keepdims=True))
            out.append((m_new, jnp.exp2(m - m_new),
                        jnp.exp2(sa - m_new).astype(BF16), jnp.exp2(sb - m_new).astype(BF16)))
        return tuple(out)

    def accumulate(kj, probs, accs):
        vb = vb_scr[pl.ds(pl.multiple_of(kj * tk, tk), tk), :]
        return tuple(alpha * acc + _dot(pa, vb[0:tq]) + _dot(pb, vb[tq:tk])
                     for (_, alpha, pa, pb), acc in zip(probs, accs))

    def trips(width, first, count, carry):
        def trip(i, carry):
            kjs = [first + i * width + u for u in range(width)]
            sc = [scores(kj) for kj in kjs]
            ms, accs = [c[0] for c in carry], [c[1] for c in carry]
            for kj, s in zip(kjs, sc):
                probs = softmax(s, ms)
                ms = [p[0] for p in probs]
                accs = accumulate(kj, probs, accs)
            return tuple(zip(ms, accs))

        return lax.fori_loop(0, count, trip, carry), first + count * width

    init = (jnp.full((tq, 1), NEG, F32), jnp.zeros((tq, 2 * dv), F32))
    carry, done = (init,) * len(chains), 0
    for width in DIFF_TRIP_WIDTHS:
        carry, done = trips(width, done, (n_chunks - done) // width, carry)
    lam = _lam_from(lamc_ref[...], lam_init)
    for r in range(2):
        (_, acc0), (_, acc1) = carry[2 * r], carry[2 * r + 1]
        o = acc0[:, 0:dv] / acc0[:, dv:dv + 1] - lam * (acc1[:, 0:dv] / acc1[:, dv:dv + 1])
        oc = _rms(o, subln_ref[...]) * (1.0 - lam_init)
        o_ref[0, r * tq:(r + 1) * tq, :] = oc * _silu(g_ref[0, r * tq:(r + 1) * tq, :])


def _diff_bias_tiles(rel_bias, tq, n_const, n_heads):
    dist = (np.arange(-1, n_const + 1)[:, None, None] * tq + np.arange(tq)[None, :, None]
            - np.arange(tq)[None, None, :]).reshape((n_const + 2) * tq, tq)
    tiles = _table_by_column(rel_bias, _bucket_code(np.clip(dist, 0, None), dist >= 0))
    return tiles.reshape(2, n_heads, n_const + 2, tq, tq)


def _const_tile_index(tq):
    exact = N_BUCKETS // 2
    sat = math.ceil(exact * (T5_MAX_DIST / exact) ** ((N_BUCKETS - 1 - exact) / (N_BUCKETS - exact))) + 1
    return -(-(sat + tq - 1) // tq)


def _diff_attn_prompt_call(q, k, v, g, rel_bias, subln, lamc, lam_init, tq):
    b, s, w = q.shape
    hw = 2 * DH_C
    n_heads = w // hw
    n_const = _const_tile_index(tq)
    assert s % (2 * tq) == 0
    tiles = _diff_bias_tiles(rel_bias * LOG2E, tq, n_const, n_heads)
    qspec = pl.BlockSpec((1, 2 * tq, hw), lambda i, h, j, *_: (i, j, h))
    kvspec = pl.BlockSpec((1, s, hw), lambda i, h, j, *_: (i, 0, h), pipeline_mode=pl.Buffered(1))
    in_specs = [qspec, kvspec, kvspec, qspec,
                pl.BlockSpec((2, None, n_const + 2, tq, tq), lambda i, h, j, *_: (0, h, 0, 0, 0),
                             pipeline_mode=pl.Buffered(1)),
                pl.BlockSpec((1, hw), lambda i, h, j, *_: (0, 0)),
                pl.BlockSpec((4, DH_C), lambda i, h, j, *_: (0, 0))]
    return ((b, n_heads, s // (2 * tq)),
            functools.partial(_diff_attn_body, tq=tq, n_const=n_const, lam_init=lam_init),
            (q, k, v, g, tiles, subln.reshape(1, hw), lamc), in_specs, qspec,
            jax.ShapeDtypeStruct((b, s, w), F32), [pltpu.VMEM((s, hw), BF16), pltpu.VMEM((s, 2 * hw), BF16)])


def _hgrn2_gates(q, f_logit, lb, live):
    f = lb + (1.0 - lb) * _sigmoid(f_logit)
    log_f, kk = jnp.log(f), 1.0 - f
    if live is not None:
        log_f, kk = jnp.where(live, log_f, 0.0), jnp.where(live, kk, 0.0)
    return _silu(q) * (DK_D ** -0.5), kk, log_f


def _hgrn2_body(q_ref, f_ref, v_ref, g_ref, lb_ref, gn_ref, s0_ref, y_ref, sl_ref, st_scr, o_scr,
                *, tb, chunk, n_heads):
    j = pl.program_id(1)

    @pl.when(j == 0)
    def _():
        for h in range(n_heads):
            st_scr[h] = s0_ref[0, h].T

    row = lax.broadcasted_iota(jnp.int32, (tb, DK_D), 0)
    pos = row % chunk
    r2 = lax.broadcasted_iota(jnp.int32, (tb, tb), 0)
    c2 = lax.broadcasted_iota(jnp.int32, (tb, tb), 1)
    same_chunk_causal = (r2 // chunk == c2 // chunk) & (c2 <= r2)
    heads = []
    for h in range(n_heads):
        ln = slice(h * DK_D, (h + 1) * DK_D)
        qf, kk, log_f = _hgrn2_gates(q_ref[0, :, ln], f_ref[0, :, ln], lb_ref[:, ln], None)
        v = v_ref[0, :, ln].astype(BF16)
        bc = log_f
        tot = log_f
        s = 1
        while s < chunk:
            bc = jnp.where(pos >= s, bc + pltpu.roll(bc, s, 0), bc)
            tot = tot + jnp.where((pos & s) == 0, pltpu.roll(tot, tb - s, 0), pltpu.roll(tot, s, 0))
            s *= 2
        qd = (qf * jnp.exp(bc)).astype(BF16)
        kd = (kk * jnp.exp(-bc)).astype(BF16)
        kl = (kk * jnp.exp(tot - bc)).astype(BF16)
        att = jnp.where(same_chunk_causal, _dot_nt(qd, kd), 0.0)
        o_scr[:, ln] = _dot(att.astype(BF16), v)
        heads.append((ln, qd, kl, v, jnp.exp(tot), st_scr[h]))

    for n in range(tb // chunk):
        rows = slice(n * chunk, (n + 1) * chunk)
        for h, (ln, qd, kl, v, gdec, st) in enumerate(heads):
            o_scr[rows, ln] += _dot_nt(qd[rows], st.astype(BF16))
            heads[h] = (ln, qd, kl, v, gdec, st * gdec[n * chunk:n * chunk + 1, :] + _dot_tn(v[rows], kl[rows]))

    for h, (ln, _, _, _, _, st) in enumerate(heads):
        st_scr[h] = st
        y_ref[0, :, ln] = _rms(o_scr[:, ln], gn_ref[...]) * _silu(g_ref[0, :, ln])

    @pl.when(j == pl.num_programs(1) - 1)
    def _():
        for h in range(n_heads):
            sl_ref[0, h] = st_scr[h].T


def _hgrn2_prompt(q, f, v, g, lb, gnorm, s0, tb):
    b, s, w = q.shape
    n_heads = w // DK_D
    seq = pl.BlockSpec((1, tb, w), lambda i, j: (i, j, 0))
    st = pl.BlockSpec((1, n_heads, DK_D, DV_D), lambda i, j: (i, 0, 0, 0))
    return pl.pallas_call(
        functools.partial(_hgrn2_body, tb=tb, chunk=CHUNK_D, n_heads=n_heads),
        grid=(b, s // tb),
        in_specs=[seq, seq, seq, seq,
                  pl.BlockSpec((1, w), lambda i, j: (0, 0)),
                  pl.BlockSpec((1, DV_D), lambda i, j: (0, 0)),
                  st],
        out_specs=[seq, st],
        out_shape=[jax.ShapeDtypeStruct((b, s, w), F32),
                   jax.ShapeDtypeStruct((b, n_heads, DK_D, DV_D), F32)],
        scratch_shapes=[pltpu.VMEM((n_heads, DV_D, DK_D), F32), pltpu.VMEM((tb, w), F32)],
        compiler_params=_cparams(("parallel", "arbitrary"), 40),
        name="hgrn2_prompt",
    )(q, f, v, g, lb.reshape(1, w), gnorm.reshape(1, DV_D), s0)


ROWS8 = 8


def _dil_decode_body(q_ref, kn_ref, vn_ref, gb_ref, kt_ref, vt_ref, tab_ref, tabn_ref, y_ref, *, n_heads):
    n_pat = len(DIL_PATTERNS)
    for s_i in range(q_ref.shape[0]):
        for h in range(n_heads):
            q = (q_ref[s_i, h] * (DH_B ** -0.5)).astype(BF16)
            s = _dot(q, kt_ref[s_i, h].astype(BF16))
            sn = _dot_nt(q, kn_ref[s_i, h].astype(BF16))
            parts = [s + tab_ref[h, g * ROWS8:(g + 1) * ROWS8] for g in range(n_pat)]
            parts += [sn + tabn_ref[h, g * ROWS8:(g + 1) * ROWS8, 0:ROWS8] for g in range(n_pat)]
            m = parts[0].max(axis=-1, keepdims=True)
            for p in parts[1:]:
                m = jnp.maximum(m, p.max(axis=-1, keepdims=True))
            ps = [jnp.exp(p - m) for p in parts]
            l = ps[0].sum(axis=-1, keepdims=True)
            for p in ps[1:]:
                l = l + p.sum(axis=-1, keepdims=True)
            pw = (ps[0] + ps[1] + ps[2]).astype(BF16)
            pn = (ps[3] + ps[4] + ps[5]).astype(BF16)
            o = _dot_nt(pw, vt_ref[s_i, h].astype(BF16)) + _dot(pn, vn_ref[s_i, h].astype(BF16))
            y_ref[s_i, h] = (o / l) * _silu(gb_ref[s_i, h])


def _dil_decode_call(q, k, v, gb, win_k, win_v, rel_bias, per_step):
    b, nt, n_heads, dh = q.shape
    l_buf = win_k.shape[1]
    assert nt <= ROWS8 and rel_bias.shape[1] == n_heads
    tt = np.tile(np.arange(ROWS8), len(DIL_PATTERNS))[:, None]
    dil = np.repeat([d for _, d in DIL_PATTERNS], ROWS8)[:, None]
    live = tt < nt

    def code(dd, lo):
        return _bucket_code(np.clip(dd, 0, None), live & (dd % dil == 0) & (dd >= lo * dil) & (dd <= BAND * dil))

    tab = _table_by_column(rel_bias, code(l_buf + tt - np.arange(l_buf)[None, :], 1))
    tn = np.arange(128)[None, :]
    tabn = _table_by_column(rel_bias, np.where(tn < nt, code(tt - tn, 0), -1).astype(np.int32))

    by_head = lambda z: jnp.pad(jnp.transpose(z, (0, 2, 1, 3)), ((0, 0), (0, 0), (0, ROWS8 - nt), (0, 0)))
    pos_minor = lambda z: jnp.transpose(z, (0, 2, 3, 1))
    new = pl.BlockSpec((per_step, n_heads, ROWS8, dh), lambda i: (i, 0, 0, 0))
    win = pl.BlockSpec((per_step, n_heads, dh, l_buf), lambda i: (i, 0, 0, 0))
    const = lambda a: pl.BlockSpec(a.shape, lambda i: (0,) * a.ndim)
    args = (by_head(q), by_head(k), by_head(v), by_head(gb), pos_minor(win_k), pos_minor(win_v), tab, tabn)
    finish = lambda y: jnp.transpose(y[:, :, :nt], (0, 2, 1, 3))
    return (functools.partial(_dil_decode_body, n_heads=n_heads), args,
            [new, new, new, new, win, win, const(tab), const(tabn)], new,
            jax.ShapeDtypeStruct((b, n_heads, ROWS8, dh), F32), finish)


def _fused_body(*refs, first, second, n_in, n_out):
    ins1, ins2 = refs[:n_in[0]], refs[n_in[0]:sum(n_in)]
    outs = refs[sum(n_in):sum(n_in) + sum(n_out)]
    first(*ins1, *outs[:n_out[0]], *refs[sum(n_in) + sum(n_out):])
    second(*ins2, *outs[n_out[0]:])


def _rglru_with_window_decode(rglru, decode):
    grid, r_body, r_args, r_in, r_out, r_shapes, scratch = _rglru_prompt_call(*rglru)
    steps = math.prod(grid)
    b_dec = decode[0].shape[0]
    assert b_dec % steps == 0, "decode sequences must split evenly over the scan's grid steps"
    d_body, d_args, d_in, d_out, d_shape, finish = _dil_decode_call(*decode, b_dec // steps)
    at_step = lambda sp: pl.BlockSpec(sp.block_shape, lambda i, j, f=sp.index_map: f(i * grid[1] + j))
    outs = pl.pallas_call(
        functools.partial(_fused_body, first=r_body, second=d_body, n_in=(len(r_in), len(d_in)),
                          n_out=(len(r_out), 1)),
        grid=grid,
        in_specs=list(r_in) + [at_step(sp) for sp in d_in],
        out_specs=list(r_out) + [at_step(d_out)],
        out_shape=list(r_shapes) + [d_shape],
        scratch_shapes=scratch,
        compiler_params=_cparams(("arbitrary", "arbitrary"), 56),
        name="rglru_window_decode",
    )(*r_args, *d_args)
    return outs[:3], finish(outs[3])


def _diff_decode_body(pt_ref, qm_ref, kn_ref, vn_ref, g_ref, tab_ref, tabn_ref, subln_ref, lamc_ref, *rest,
                      nt, n_heads, n_pages, lam_init):
    del pt_ref
    k_pages, v_pages, y_ref = rest[:n_pages], rest[n_pages:2 * n_pages], rest[2 * n_pages]
    qm = (qm_ref[0] * (DH_C ** -0.5)).astype(BF16)
    rows, hw = qm.shape
    grp = 2 * n_heads
    pw = PAGE * n_heads
    ss = [_dot_nt(qm, k_pages[p][0].astype(BF16)) + tab_ref[:, p * pw:(p + 1) * pw] for p in range(n_pages)]
    sn = _dot_nt(qm, kn_ref[0].astype(BF16)) + tabn_ref[:, 0:kn_ref.shape[1]]
    m = sn.max(axis=-1, keepdims=True)
    for s in ss:
        m = jnp.maximum(m, s.max(axis=-1, keepdims=True))
    ps = [jnp.exp(s - m) for s in ss]
    pn = jnp.exp(sn - m)
    l = pn.sum(axis=-1, keepdims=True)
    for p in ps:
        l = l + p.sum(axis=-1, keepdims=True)
    lam = _lam_from(lamc_ref[...], lam_init)
    row_map = (lax.broadcasted_iota(jnp.int32, (rows, 1), 0) % grp) // n_heads
    coef = jnp.where(row_map == 0, 1.0, -lam) / l
    o = _dot((pn * coef).astype(BF16), vn_ref[0].astype(BF16))
    for p in range(n_pages):
        o = o + _dot((ps[p] * coef).astype(BF16), v_pages[p][0].astype(BF16))
    o = o.reshape(nt, grp, hw)
    oc = o[:, 0:n_heads, :] + o[:, n_heads:grp, :]
    y_ref[0] = _rms(oc, subln_ref[...]) * (1.0 - lam_init) * _silu(g_ref[0])


def _diff_decode_call(q, k, v, g, cache_k, cache_v, page_table, rel_bias, subln, lamc, lam_init):
    b, nt, w = q.shape
    hw = 2 * DH_C
    n_heads = w // hw
    n_phys = cache_k.shape[0]
    n_pages = page_table.shape[1]
    past = n_pages * PAGE
    grp = 2 * n_heads
    rows = nt * grp
    assert rel_bias.shape[1] == grp
    tt = np.arange(rows)[:, None] // grp
    hq = np.arange(rows)[:, None] % n_heads
    pos = lambda n: np.arange(n)[None, :] // n_heads
    same = lambda n: np.arange(n)[None, :] % n_heads == hq
    tab = _table_by_row(rel_bias, _bucket_code(past + tt - pos(past * n_heads), same(past * n_heads)))
    jn = tt - pos(128)
    tabn = _table_by_row(rel_bias, _bucket_code(np.clip(jn, 0, None), same(128) & (jn >= 0) & (pos(128) < nt)))

    map_mask = (np.arange(hw)[None, :] // DH_C == np.arange(2)[:, None]).astype(np.float32)
    qm = q.reshape(b, nt, 1, n_heads, hw) * map_mask[None, None, :, None, :]
    by_head = lambda z: z.reshape(b, nt * n_heads, hw)

    per_b = lambda *shape: pl.BlockSpec((1,) + shape, lambda i, pt: (i,) + (0,) * len(shape))
    const = lambda *shape: pl.BlockSpec(shape, lambda i, pt: (0,) * len(shape))
    page = lambda p: pl.BlockSpec((1, PAGE * n_heads, hw), lambda i, pt, p=p: (pt[i * n_pages + p], 0, 0))
    in_specs = ([per_b(rows, hw), per_b(nt * n_heads, hw), per_b(nt * n_heads, hw), per_b(nt, n_heads, hw),
                 const(rows, past * n_heads), const(rows, 128), const(1, hw), const(4, DH_C)]
                + [page(p) for p in range(n_pages)] * 2)
    rows_of = lambda c: c.reshape(n_phys, PAGE * n_heads, hw)
    args = (qm.reshape(b, rows, hw), by_head(k), by_head(v), g.reshape(b, nt, n_heads, hw),
            tab, tabn, subln.reshape(1, hw), lamc, *([rows_of(cache_k)] * n_pages), *([rows_of(cache_v)] * n_pages))
    return (functools.partial(_diff_decode_body, nt=nt, n_heads=n_heads, n_pages=n_pages, lam_init=lam_init),
            args, in_specs, per_b(nt, n_heads, hw), jax.ShapeDtypeStruct((b, nt, n_heads, hw), F32))


def _diff_attn_both_body(pt_ref, *refs, prompt_body, decode_body, n_prompt_in, n_decode_in):
    n_in = n_prompt_in + n_decode_in
    o_ref, y_ref = refs[n_in], refs[n_in + 1]
    prompt_body(*refs[:n_prompt_in], o_ref, *refs[n_in + 2:])
    decode_body(pt_ref, *refs[n_prompt_in:n_in], y_ref)


def _diff_attn(prompt, decode, page_table, common):
    grid, p_body, p_args, p_specs, p_out, p_shape, scratch = _diff_attn_prompt_call(*prompt, *common)
    d_body, d_args, d_specs, d_out, d_shape = _diff_decode_call(*decode, page_table, *common[:-1])
    assert d_shape.shape[0] == math.prod(grid), "one decode sequence per prompt grid step"

    def at_step(spec):
        seq = lambda i, h, j, pt: (i * grid[1] + h) * grid[2] + j
        return pl.BlockSpec(spec.block_shape, lambda i, h, j, pt, f=spec.index_map: f(seq(i, h, j, pt), pt))

    grid_spec = pltpu.PrefetchScalarGridSpec(
        num_scalar_prefetch=1, grid=grid,
        in_specs=list(p_specs) + [at_step(sp) for sp in d_specs],
        out_specs=[p_out, at_step(d_out)],
        scratch_shapes=scratch)
    return pl.pallas_call(
        functools.partial(_diff_attn_both_body, prompt_body=p_body, decode_body=d_body,
                          n_prompt_in=len(p_specs), n_decode_in=len(d_specs)),
        grid_spec=grid_spec,
        out_shape=[p_shape, d_shape],
        compiler_params=_cparams(("arbitrary", "arbitrary", "arbitrary"), 60),
        name="diff_attn",
    )(page_table.reshape(-1), *p_args, *d_args)


def _hgrn2_decode_body(q_ref, f_ref, ft_ref, v_ref, g_ref, lb_ref, lbc_ref, gn_ref, s0_ref, y_ref, sl_ref,
                       *, nt, n_heads):
    row = lax.broadcasted_iota(jnp.int32, (ROWS8, DK_D), 0)
    r2 = lax.broadcasted_iota(jnp.int32, (ROWS8, ROWS8), 0)
    c2 = lax.broadcasted_iota(jnp.int32, (ROWS8, ROWS8), 1)
    for h in range(n_heads):
        ln = slice(h * DK_D, (h + 1) * DK_D)
        qf, kk, log_f = _hgrn2_gates(q_ref[0, :, ln], f_ref[0, :, ln], lb_ref[:, ln], row < nt)
        v = v_ref[0, :, ln].astype(BF16)
        bc = log_f
        s = 1
        while s < ROWS8:
            bc = jnp.where(row >= s, bc + pltpu.roll(bc, s, 0), bc)
            s *= 2
        tot = bc[ROWS8 - 1:ROWS8, :]
        qd = (qf * jnp.exp(bc)).astype(BF16)
        kd = (kk * jnp.exp(-bc)).astype(BF16)
        kl = (kk * jnp.exp(tot - bc)).astype(BF16)
        s0 = s0_ref[0, h]
        att = jnp.where(c2 <= r2, _dot_nt(qd, kd), 0.0)
        o = _dot(att.astype(BF16), v) + _dot(qd, s0.astype(BF16))
        lbc = lbc_ref[h]
        fc = lbc + (1.0 - lbc) * _sigmoid(ft_ref[0, h])
        decay = fc[:, 0:1]
        for t in range(1, nt):
            decay = decay * fc[:, t:t + 1]
        sl_ref[0, h] = decay * s0 + _dot_tn(kl, v)
        y_ref[0, :, ln] = _rms(o, gn_ref[...]) * _silu(g_ref[0, :, ln])


def _hgrn2_decode(q, f, v, g, lb, gnorm, s0):
    b, nt, w = q.shape
    n_heads = w // DK_D
    assert nt <= ROWS8
    pad = lambda z: jnp.pad(z, ((0, 0), (0, ROWS8 - nt), (0, 0)))
    cols = jnp.transpose(f.reshape(b, nt, n_heads, DK_D), (0, 2, 3, 1))
    per_b = lambda *shape: pl.BlockSpec((1,) + shape, lambda i: (i,) + (0,) * len(shape))
    const = lambda *shape: pl.BlockSpec(shape, lambda i: (0,) * len(shape))
    y, s_last = pl.pallas_call(
        functools.partial(_hgrn2_decode_body, nt=nt, n_heads=n_heads),
        grid=(b,),
        in_specs=[per_b(ROWS8, w), per_b(ROWS8, w), per_b(n_heads, DK_D, nt), per_b(ROWS8, w), per_b(ROWS8, w),
                  const(1, w), const(n_heads, DK_D, 1), const(1, DV_D), per_b(n_heads, DK_D, DV_D)],
        out_specs=[per_b(ROWS8, w), per_b(n_heads, DK_D, DV_D)],
        out_shape=[jax.ShapeDtypeStruct((b, ROWS8, w), F32),
                   jax.ShapeDtypeStruct((b, n_heads, DK_D, DV_D), F32)],
        compiler_params=_cparams(("parallel",), 32),
        name="hgrn2_decode",
    )(pad(q), pad(f), cols, pad(v), pad(g), lb.reshape(1, w), lb.reshape(n_heads, DK_D, 1), gnorm.reshape(1, DV_D),
      s0)
    return y[:, :nt], s_last


def _layer_ab(xp, xs, group, w_in, gate_params, w_out, rel_bias, state):
    c = w_out.shape[0] // 2
    by_head = lambda z: z.reshape(z.shape[0], z.shape[1], c // DH_B, DH_B)

    def project(x):
        b, t, d = x.shape
        return [z.reshape(b, t, c) for z in _norm_proj(x.reshape(b * t, d), *group, w_in, (c,) * 6, min(512, b * t))]

    def finish(x, ya, yb):
        b, t, d = x.shape
        y = _out_proj(x.reshape(b * t, d), ya.reshape(b * t, c), yb.reshape(b * t, c), w_out, None, min(512, b * t))
        return y.reshape(b, t, d)

    bp, tp, _ = xp.shape
    xa, ga, q, k, v, gb = project(xp)
    xa2, ga2, q2, k2, v2, gb2 = project(xs)
    conv_buf, h0, win_k, win_v = state
    k_rows2, v_rows2 = by_head(k2), by_head(v2)
    (ya, conv_new, h_last), yb2 = _rglru_with_window_decode(
        (xa, ga, jnp.zeros((bp, CONV_W - 1, c), F32), jnp.zeros((bp, c), F32), gate_params, 256),
        (by_head(q2), k_rows2, v_rows2, by_head(gb2), win_k, win_v, rel_bias))
    yb = _dil_prompt(q, k, v, gb, rel_bias)
    ya2, conv_new2, h_last2 = _rglru_decode(xa2, ga2, conv_buf, h0, gate_params)
    keep = min(WIN_MAX, tp)
    return ((finish(xp, ya, yb), conv_new, h_last.reshape(bp, c), by_head(k[:, tp - keep:]), by_head(v[:, tp - keep:])),
            (finish(xs, ya2, yb2), conv_new2, h_last2, k_rows2, v_rows2))


def _layer_cd(xp, xs, group, w_in, lamc, subln, lb, gnorm, w_out, rel_bias, lam_init, final_g, state):
    c = w_out.shape[0] // 2
    heads = c // (2 * DH_C)

    def project(x):
        b, t, d = x.shape
        return [z.reshape(b, t, c) for z in _norm_proj(x.reshape(b * t, d), *group, w_in, (c,) * 8, min(512, b * t))]

    def finish(x, yc, yd, kc, vc, s_last):
        b, t, d = x.shape
        y = _out_proj(x.reshape(b * t, d), yc.reshape(b * t, c), yd.reshape(b * t, c), w_out, final_g,
                      min(512, b * t))
        return (y.reshape(b, t, d), kc.reshape(b, t, heads, 2 * DH_C), vc.reshape(b, t, heads, 2 * DH_C), s_last)

    qc, kc, vc, gc, qd, fd, idd, gd = project(xp)
    qc2, kc2, vc2, gc2, qd2, fd2, idd2, gd2 = project(xs)
    s0, cache_k, cache_v, page_table = state
    yc, yc2 = _diff_attn((qc, kc, vc, gc), (qc2, kc2, vc2, gc2, cache_k, cache_v), page_table,
                         (rel_bias, subln, lamc, lam_init, 256))
    yd, s_last = _hgrn2_prompt(qd, fd, idd, gd, lb, gnorm, jnp.zeros((xp.shape[0], c // DK_D, DK_D, DV_D), F32),
                               tb=256)
    yd2, s_last2 = _hgrn2_decode(qd2, fd2, idd2, gd2, lb, gnorm, s0)
    return finish(xp, yc, yd, kc, vc, s_last), finish(xs, yc2, yd2, kc2, vc2, s_last2)


def kernel(x_prompt, x_sample, state_conv_a, state_h_a, cache_win_k, cache_win_v, cache_k_c, cache_v_c, state_s_d,
           page_table, norm_g, norm_final, rel_bias, w_in_ab, conv_w_a, conv_b_a, w_r_a, b_r_a, w_i_a, b_i_a, lam_a,
           w_out_ab, w_in_cd, lam_c, subln_c, lb_d, gnorm_d, w_out_cd):
    depth = norm_g.shape[0]
    assert depth == 2
    lb_soft = jax.nn.softmax(lb_d.astype(F32), axis=0)
    lb_all = jnp.cumsum(lb_soft, axis=0) - lb_soft[0]

    gate_params = _gate_params(conv_w_a[0], conv_b_a[0], w_r_a[0], b_r_a[0], w_i_a[0], b_i_a[0], lam_a[0])
    (xp, conv_p, h_p, wk_p, wv_p), (xs, conv_s, h_s, wk_s, wv_s) = _layer_ab(
        x_prompt, x_sample, (norm_g[0],), w_in_ab[0], gate_params, w_out_ab[0], rel_bias,
        (state_conv_a[0], state_h_a[0], cache_win_k[0], cache_win_v[0]))
    lam_init = 0.8 - 0.6 * math.exp(-0.3 * 1)
    cd = (w_in_cd[0], lam_c[0], subln_c[0], lb_all[1], gnorm_d[0], w_out_cd[0], rel_bias, lam_init, norm_final)
    (yp, kc_p, vc_p, s_p), (ys, kc_s, vc_s, s_s) = _layer_cd(
        xp, xs, (norm_g[1],), *cd, (state_s_d[0], cache_k_c[0], cache_v_c[0], page_table))
    e = lambda z: z[None]
    return (yp, ys, e(conv_p), e(h_p), e(wk_p), e(wv_p), e(kc_p), e(vc_p), e(s_p),
            e(conv_s), e(h_s), e(wk_s), e(wv_s), e(kc_s), e(vc_s), e(s_s))
```

```python
import functools
import math

import numpy as np
import jax
import jax.numpy as jnp
from jax import lax
from jax.experimental import pallas as pl
from jax.experimental.pallas import tpu as pltpu

F32 = jnp.float32
BF16 = jnp.bfloat16

N_BLK_A = 8
CONV_W = 4
RG_C = 8.0
DH_B = 64
DIL_PATTERNS = ((128, 1), (512, 4), (2048, 16))
BAND = 128
WIN_MAX = 2048
DH_C = 64
DK_D = 128
DV_D = 128
CHUNK_D = 16
N_BUCKETS = 32
T5_MAX_DIST = 2048
PAGE = 128
EPS = 1e-6
NEG = -1e30
MIB = 2 ** 20


def _cparams(sem, vmem_mib):
    return pltpu.CompilerParams(dimension_semantics=sem, vmem_limit_bytes=vmem_mib * MIB)


def _dot(a, b):
    return jnp.dot(a, b, preferred_element_type=F32)


def _dot_nt(a, b):
    return lax.dot_general(a, b, (((1,), (1,)), ((), ())), preferred_element_type=F32)


def _dot_tn(a, b):
    return lax.dot_general(a, b, (((0,), (0,)), ((), ())), preferred_element_type=F32)


def _sigmoid(x):
    return 1.0 / (1.0 + jnp.exp(-x))


def _silu(x):
    return x * _sigmoid(x)


def _rms(x, g):
    return x * lax.rsqrt(jnp.mean(x * x, axis=-1, keepdims=True) + EPS) * g


def _t5_bucket(dist):
    n = np.maximum(np.asarray(dist), 0)
    exact = N_BUCKETS // 2
    nf = np.maximum(n, 1).astype(np.float32)
    val = np.log(nf / np.float32(exact)) / np.float32(math.log(T5_MAX_DIST / exact)) * np.float32(N_BUCKETS - exact)
    large = np.minimum(exact + val.astype(np.int32), N_BUCKETS - 1)
    return np.where(n < exact, n, large).astype(np.int32)


def _norm_proj_body(x_ref, g_ref, w_ref, *o_refs):
    hn = _rms(x_ref[...], g_ref[...]).astype(BF16)
    off = 0
    for o_ref in o_refs:
        w = o_ref.shape[-1]
        o_ref[...] = _dot(hn, w_ref[:, off:off + w])
        off += w


def _norm_proj(x, g, w, widths, tm):
    t, d = x.shape
    n = w.shape[1]
    return pl.pallas_call(
        _norm_proj_body,
        grid=(t // tm,),
        in_specs=[pl.BlockSpec((tm, d), lambda i: (i, 0)),
                  pl.BlockSpec((1, d), lambda i: (0, 0)),
                  pl.BlockSpec((d, n), lambda i: (0, 0))],
        out_specs=[pl.BlockSpec((tm, wd), lambda i: (i, 0)) for wd in widths],
        out_shape=[jax.ShapeDtypeStruct((t, wd), F32) for wd in widths],
        compiler_params=_cparams(("parallel",), 56),
        name="norm_proj",
    )(x, g.reshape(1, d), w.astype(BF16))


def _rglru_gates(xc, wr, br, wi, bi, lam):
    xcb = xc.astype(BF16)
    r = _sigmoid(_dot(xcb, wr) + br)
    gi = _sigmoid(_dot(xcb, wi) + bi)
    nl = -lam
    softplus = jnp.maximum(nl, 0.0) + jnp.log(1.0 + jnp.exp(-jnp.abs(nl)))
    log_a = (-RG_C) * r * softplus
    a = jnp.exp(log_a)
    u = jnp.sqrt(1.0 - jnp.exp(2.0 * log_a)) * (gi * xc)
    return a, u


def _rglru_prompt_body(xa_ref, ga_ref, cb_ref, h0_ref, cw_ref, cbias_ref, wr_ref, br_ref, wi_ref, bi_ref,
                       lam_ref, ya_ref, cnew_ref, hl_ref, xp_scr, h_scr, *, tb):
    j = pl.program_id(1)

    @pl.when(j == 0)
    def _():
        xp_scr[5:8, :] = cb_ref[0]
        h_scr[...] = h0_ref[0]

    xp_scr[8:8 + tb, :] = xa_ref[0]
    cw = cw_ref[...]
    xc = cbias_ref[...] + xp_scr[5:5 + tb, :] * cw[0:1, :]
    for k in range(1, CONV_W):
        xc = xc + xp_scr[5 + k:5 + k + tb, :] * cw[k:k + 1, :]
    tail = xp_scr[5 + tb:8 + tb, :]
    cnew_ref[0] = tail
    xp_scr[5:8, :] = tail

    a, u = _rglru_gates(xc, wr_ref[...], br_ref[...], wi_ref[...], bi_ref[...], lam_ref[...])
    row = lax.broadcasted_iota(jnp.int32, a.shape, 0)
    s = 1
    while s < tb:
        keep = row >= s
        u = jnp.where(keep, a * pltpu.roll(u, s, 0) + u, u)
        a = jnp.where(keep, a * pltpu.roll(a, s, 0), a)
        s *= 2
    h = a * h_scr[...] + u
    h_last = h[tb - 1:tb, :]
    h_scr[...] = h_last
    hl_ref[0] = h_last
    ya_ref[0] = h * _silu(ga_ref[0])


def _rglru_decode_body(xa_ref, ga_ref, cb_ref, h0_ref, cw_ref, cbias_ref, wr_ref, br_ref, wi_ref, bi_ref,
                       lam_ref, ya_ref, cnew_ref, hl_ref):
    nt = xa_ref.shape[0]
    xp = [cb_ref[i] for i in range(CONV_W - 1)] + [xa_ref[t] for t in range(nt)]
    cw = cw_ref[...]
    h = h0_ref[...]
    for t in range(nt):
        xc = cbias_ref[...] + xp[t] * cw[0:1, :]
        for k in range(1, CONV_W):
            xc = xc + xp[t + k] * cw[k:k + 1, :]
        a, u = _rglru_gates(xc, wr_ref[...], br_ref[...], wi_ref[...], bi_ref[...], lam_ref[...])
        h = a * h + u
        ya_ref[t] = h * _silu(ga_ref[t])
    for i in range(CONV_W - 1):
        cnew_ref[i] = xp[nt + i]
    hl_ref[...] = h


def _block_diag(w):
    g, bi, bj = w.shape
    eye = jnp.eye(g, dtype=w.dtype)
    return (w[:, :, None, :] * eye[:, None, :, None]).reshape(g * bi, g * bj)


def _gate_params(conv_w, conv_b, w_r, b_r, w_i, b_i, lam):
    c = conv_w.shape[-1]
    return (conv_w, conv_b.reshape(1, c), _block_diag(w_r).astype(BF16), b_r.reshape(1, c),
            _block_diag(w_i).astype(BF16), b_i.reshape(1, c), lam.reshape(1, c))


def _rglru_prompt_call(xa, ga, conv_buf, h0, gate_params, tb):
    b, t, c = xa.shape
    full = lambda shape: pl.BlockSpec(shape, lambda i, j: (0,) * len(shape))
    seq = pl.BlockSpec((1, tb, c), lambda i, j: (i, j, 0))
    in_specs = [seq, seq,
                pl.BlockSpec((1, CONV_W - 1, c), lambda i, j: (i, 0, 0)),
                pl.BlockSpec((1, 1, c), lambda i, j: (i, 0, 0)),
                full((CONV_W, c)), full((1, c)), full((c, c)), full((1, c)), full((c, c)), full((1, c)),
                full((1, c))]
    out_specs = [seq,
                 pl.BlockSpec((1, CONV_W - 1, c), lambda i, j: (i, 0, 0)),
                 pl.BlockSpec((1, 1, c), lambda i, j: (i, 0, 0))]
    out_shapes = [jax.ShapeDtypeStruct((b, t, c), F32),
                  jax.ShapeDtypeStruct((b, CONV_W - 1, c), F32),
                  jax.ShapeDtypeStruct((b, 1, c), F32)]
    return ((b, t // tb), functools.partial(_rglru_prompt_body, tb=tb),
            (xa, ga, conv_buf, h0.reshape(b, 1, c), *gate_params), in_specs, out_specs, out_shapes,
            [pltpu.VMEM((tb + 8, c), F32), pltpu.VMEM((1, c), F32)])


def _rglru_decode(xa, ga, conv_buf, h0, gate_params):
    b, t, c = xa.shape
    tm = lambda z: jnp.transpose(z, (1, 0, 2))
    ya, cnew, hl = pl.pallas_call(
        _rglru_decode_body,
        out_shape=[jax.ShapeDtypeStruct((t, b, c), F32),
                   jax.ShapeDtypeStruct((CONV_W - 1, b, c), F32),
                   jax.ShapeDtypeStruct((b, c), F32)],
        compiler_params=pltpu.CompilerParams(vmem_limit_bytes=40 * MIB),
        name="rglru_decode",
    )(tm(xa), tm(ga), tm(conv_buf), h0, *gate_params)
    return tm(ya), tm(cnew), hl


TABLE_VREGS = 16


def _lane_chunk(x):
    return next(c for c in (2048, 1024, 512, 256, 128) if x % c == 0)


def _table_by_column_body(vals_ref, code_ref, out_ref, *, rc):
    base = pl.program_id(0) * N_BUCKETS

    def chunk(ci, carry):
        r0 = pl.multiple_of(ci * rc, rc)
        code = code_ref[pl.ds(r0, rc), :]
        look = lambda k, out: jnp.where(code == k, vals_ref[base + k], out)
        out_ref[0, pl.ds(r0, rc), :] = lax.fori_loop(0, N_BUCKETS, look, jnp.full(code.shape, NEG, F32), unroll=8)
        return carry

    lax.fori_loop(0, code_ref.shape[0] // rc, chunk, 0)


def _table_by_column(rel_bias, code):
    r, x = code.shape
    n_col = rel_bias.shape[1]
    rc = min(r, max(8, TABLE_VREGS * 1024 // x))
    assert r % rc == 0
    return pl.pallas_call(
        functools.partial(_table_by_column_body, rc=rc),
        grid=(n_col,),
        in_specs=[pl.BlockSpec(memory_space=pltpu.SMEM), pl.BlockSpec((r, x), lambda g: (0, 0))],
        out_specs=pl.BlockSpec((1, r, x), lambda g: (g, 0, 0)),
        out_shape=jax.ShapeDtypeStruct((n_col, r, x), F32),
        compiler_params=_cparams(("parallel",), 32),
        name="bias_table_by_column",
    )(rel_bias.T.reshape(-1), code)


def _table_by_row_body(rbt_ref, code_ref, out_ref, *, lc):
    rbt = rbt_ref[...]
    n_col = rbt.shape[0]

    def group(gi, carry):
        r0 = pl.multiple_of(gi * n_col, n_col)
        for x0 in range(0, code_ref.shape[1], lc):
            code = code_ref[pl.ds(r0, n_col), x0:x0 + lc]
            out = jnp.full(code.shape, NEG, F32)
            for k in range(N_BUCKETS):
                out = jnp.where(code == k, rbt[:, k:k + 1], out)
            out_ref[pl.ds(r0, n_col), x0:x0 + lc] = out
        return carry

    lax.fori_loop(0, code_ref.shape[0] // n_col, group, 0)


def _table_by_row(rel_bias, code):
    r, x = code.shape
    return pl.pallas_call(
        functools.partial(_table_by_row_body, lc=_lane_chunk(x)),
        out_shape=jax.ShapeDtypeStruct((r, x), F32),
        name="bias_table_by_row",
    )(rel_bias.T, code)


def _bucket_code(dist, valid):
    return np.where(valid, _t5_bucket(dist), -1).astype(np.int32)


DIL_GROUP = 8


def _rows(start, dil):
    return pl.ds(start, BAND, stride=dil) if dil > 1 else pl.ds(start, BAND)


def _dil_prompt_body(q_ref, k_ref, v_ref, gb_ref, bt_ref, y_ref, k_scr, v_scr, o_scr, lse_scr, *, unit, n_heads):
    n = pl.program_id(2)

    @pl.when(n == 0)
    def _():
        k_scr[0:unit, :] = jnp.zeros((unit, k_scr.shape[1]), F32)
        v_scr[0:unit, :] = jnp.zeros((unit, v_scr.shape[1]), F32)

    k_scr[unit:2 * unit, :] = k_ref[0]
    v_scr[unit:2 * unit, :] = v_ref[0]
    for g, (_, dil) in enumerate(DIL_PATTERNS):
        span = BAND * dil

        def blocks(ci, carry, g=g, dil=dil, span=span):
            lane_head = lax.broadcasted_iota(jnp.int32, (BAND, k_scr.shape[1]), 1) // DH_B
            ones = jnp.ones((BAND, k_scr.shape[1]), BF16)
            loaded = []
            for u in range(DIL_GROUP):
                c = ci * DIL_GROUP + u
                sub = c // dil
                base = sub * span + c % dil
                if dil == 1:
                    base = pl.multiple_of(base, BAND)
                loaded.append((base, jnp.logical_or(n > 0, sub > 0),
                               q_ref[0, _rows(base, dil), :] * (DH_B ** -0.5),
                               k_scr[_rows(unit + base, dil), :].astype(BF16),
                               k_scr[_rows(unit + base - span, dil), :].astype(BF16),
                               v_scr[_rows(unit + base, dil), :].astype(BF16),
                               v_scr[_rows(unit + base - span, dil), :].astype(BF16)))
            scores = []
            for _, has_prev, q, kc, kp, _, _ in loaded:
                for h in range(n_heads):
                    qh = jnp.where(lane_head == h, q, 0.0).astype(BF16)
                    scores.append((jnp.where(has_prev, _dot_nt(qh, kp) + bt_ref[h, g, :, 0:BAND], NEG),
                                   _dot_nt(qh, kc) + bt_ref[h, g, :, BAND:2 * BAND]))
            probs = []
            for sp, sc in scores:
                m = jnp.max(jnp.maximum(sp, sc), axis=-1, keepdims=True)
                probs.append((m, jnp.exp(sp - m).astype(BF16), jnp.exp(sc - m).astype(BF16)))
            for u, (base, _, _, _, _, vc, vp) in enumerate(loaded):
                o_sel = lse_sel = None
                for h in range(n_heads):
                    m, pp, pc = probs[u * n_heads + h]
                    l = _dot(pp, ones) + _dot(pc, ones)
                    o = (_dot(pp, vp) + _dot(pc, vc)) / l
                    lse = m + jnp.log(l)
                    o_sel = o if h == 0 else jnp.where(lane_head == h, o, o_sel)
                    lse_sel = lse if h == 0 else jnp.where(lane_head == h, lse, lse_sel)
                o_scr[g, _rows(base, dil), :] = o_sel
                lse_scr[g, _rows(base, dil), :] = lse_sel
            return carry

        lax.fori_loop(0, unit // BAND // DIL_GROUP, blocks, 0)

    lse1, lse2, lse3 = lse_scr[0], lse_scr[1], lse_scr[2]
    m = jnp.maximum(jnp.maximum(lse1, lse2), lse3)
    e1, e2, e3 = jnp.exp(lse1 - m), jnp.exp(lse2 - m), jnp.exp(lse3 - m)
    ob = (e1 * o_scr[0] + e2 * o_scr[1] + e3 * o_scr[2]) / (e1 + e2 + e3)
    y_ref[0] = ob * _silu(gb_ref[0])
    k_scr[0:unit, :] = k_ref[0]
    v_scr[0:unit, :] = v_ref[0]


def _dil_prompt(q, k, v, gb, rel_bias):
    b, s, w = q.shape
    unit = BAND * DIL_PATTERNS[-1][1]
    lw = 2 * DH_B
    n_heads = w // DH_B
    idx = np.arange(BAND)[:, None] + BAND - np.arange(2 * BAND)[None, :]
    ok = (idx >= 0) & (idx <= BAND)
    code = np.concatenate([_bucket_code(dil * np.clip(idx, 0, BAND), ok) for _, dil in DIL_PATTERNS])
    tiles = _table_by_column(rel_bias, code).reshape(n_heads, len(DIL_PATTERNS), BAND, 2 * BAND)
    seq = pl.BlockSpec((1, unit, lw), lambda i, hp, n: (i, n, hp))
    return pl.pallas_call(
        functools.partial(_dil_prompt_body, unit=unit, n_heads=lw // DH_B),
        grid=(b, w // lw, s // unit),
        in_specs=[seq, seq, seq, seq,
                  pl.BlockSpec((lw // DH_B, len(DIL_PATTERNS), BAND, 2 * BAND), lambda i, hp, n: (hp, 0, 0, 0))],
        out_specs=seq,
        out_shape=jax.ShapeDtypeStruct((b, s, w), F32),
        scratch_shapes=[pltpu.VMEM((2 * unit, lw), F32), pltpu.VMEM((2 * unit, lw), F32),
                        pltpu.VMEM((len(DIL_PATTERNS), unit, lw), F32),
                        pltpu.VMEM((len(DIL_PATTERNS), unit, lw), F32)],
        compiler_params=_cparams(("parallel", "parallel", "arbitrary"), 40),
        name="dil_prompt",
    )(q, k, v, gb, tiles)


def _out_proj_body(x_ref, ya_ref, yb_ref, wa_ref, wb_ref, *rest, final_norm):
    y = x_ref[...] + _dot(ya_ref[...].astype(BF16), wa_ref[...]) + _dot(yb_ref[...].astype(BF16), wb_ref[...])
    if final_norm:
        y = _rms(y, rest[0][...])
    rest[-1][...] = y


def _out_proj(x, ya, yb, w_out, final_g, tm):
    t, d = x.shape
    ca, cb = ya.shape[1], yb.shape[1]
    row = lambda c: pl.BlockSpec((tm, c), lambda i: (i, 0))
    const = lambda shape: pl.BlockSpec(shape, lambda i: (0, 0))
    wa = w_out[:ca].astype(BF16)
    wb = w_out[ca:].astype(BF16)
    args = [x, ya, yb, wa, wb]
    in_specs = [row(d), row(ca), row(cb), const((ca, d)), const((cb, d))]
    if final_g is not None:
        args.append(final_g.reshape(1, d))
        in_specs.append(const((1, d)))
    return pl.pallas_call(
        functools.partial(_out_proj_body, final_norm=final_g is not None),
        grid=(t // tm,),
        in_specs=in_specs,
        out_specs=row(d),
        out_shape=jax.ShapeDtypeStruct((t, d), F32),
        compiler_params=_cparams(("parallel",), 48),
        name="out_proj",
    )(*args)


def _lam_from(lamc, lam_init):
    s01 = jnp.sum(lamc[0:1, :] * lamc[1:2, :], axis=-1, keepdims=True)
    s23 = jnp.sum(lamc[2:3, :] * lamc[3:4, :], axis=-1, keepdims=True)
    return jnp.exp(s01) - jnp.exp(s23) + lam_init


LOG2E = math.log2(math.e)
DIFF_TRIP_WIDTHS = (4, 2, 1)


def _diff_attn_body(q_ref, k_ref, v_ref, g_ref, bt_ref, subln_ref, lamc_ref, o_ref, kb_scr, vb_scr,
                    *, tq, n_const, lam_init):
    qi = pl.program_id(2)
    dv = v_ref.shape[-1]
    tk = 2 * tq

    @pl.when(qi == 0)
    def _():
        kb_scr[...] = k_ref[0].astype(BF16)
        vb_scr[:, 0:dv] = v_ref[0].astype(BF16)
        lane = lax.broadcasted_iota(jnp.int32, (vb_scr.shape[0], dv), 1)
        vb_scr[:, dv:2 * dv] = jnp.where(lane == 0, 1.0, 0.0).astype(BF16)

    q = q_ref[0] * (DH_C ** -0.5 * LOG2E)
    lane = lax.broadcasted_iota(jnp.int32, q.shape, 1)
    qm = (jnp.where(lane < DH_C, q, 0.0).astype(BF16), jnp.where(lane >= DH_C, q, 0.0).astype(BF16))
    chains = [(r, c) for r in range(2) for c in range(2)]
    n_chunks = qi + 1

    def scores(kj):
        kb = kb_scr[pl.ds(pl.multiple_of(kj * tk, tk), tk), :]
        out = []
        for r, c in chains:
            d = 2 * (qi - kj) + r
            ia, ib = jnp.minimum(d, n_const) + 1, jnp.minimum(d - 1, n_const) + 1
            s = _dot_nt(qm[c][r * tq:(r + 1) * tq], kb)
            out.append((s[:, 0:tq] + bt_ref[c, ia], s[:, tq:tk] + bt_ref[c, ib]))
        return tuple(out)

    def softmax(sc, ms):
        out = []
        for (sa, sb), m in zip(sc, ms):
            m_new = jnp.maximum(m, jnp.max(jnp.maximum(sa, sb), axis=-1, keepdims=True))
            out.append((m_new, jnp.exp2(m - m_new),
                        jnp.exp2(sa - m_new).astype(BF16), jnp.exp2(sb - m_new).astype(BF16)))
        return tuple(out)

    def accumulate(kj, probs, accs):
        vb = vb_scr[pl.ds(pl.multiple_of(kj * tk, tk), tk), :]
        return tuple(alpha * acc + _dot(pa, vb[0:tq]) + _dot(pb, vb[tq:tk])
                     for (_, alpha, pa, pb), acc in zip(probs, accs))

    def trips(width, first, count, carry):
        def trip(i, carry):
            kjs = [first + i * width + u for u in range(width)]
            sc = [scores(kj) for kj in kjs]
            ms, accs = [c[0] for c in carry], [c[1] for c in carry]
            for kj, s in zip(kjs, sc):
                probs = softmax(s, ms)
                ms = [p[0] for p in probs]
                accs = accumulate(kj, probs, accs)
            return tuple(zip(ms, accs))

        return lax.fori_loop(0, count, trip, carry), first + count * width

    init = (jnp.full((tq, 1), NEG, F32), jnp.zeros((tq, 2 * dv), F32))
    carry, done = (init,) * len(chains), 0
    for width in DIFF_TRIP_WIDTHS:
        carry, done = trips(width, done, (n_chunks - done) // width, carry)
    lam = _lam_from(lamc_ref[...], lam_init)
    for r in range(2):
        (_, acc0), (_, acc1) = carry[2 * r], carry[2 * r + 1]
        o = acc0[:, 0:dv] / acc0[:, dv:dv + 1] - lam * (acc1[:, 0:dv] / acc1[:, dv:dv + 1])
        oc = _rms(o, subln_ref[...]) * (1.0 - lam_init)
        o_ref[0, r * tq:(r + 1) * tq, :] = oc * _silu(g_ref[0, r * tq:(r + 1) * tq, :])


def _diff_bias_tiles(rel_bias, tq, n_const, n_heads):
    dist = (np.arange(-1, n_const + 1)[:, None, None] * tq + np.arange(tq)[None, :, None]
            - np.arange(tq)[None, None, :]).reshape((n_const + 2) * tq, tq)
    tiles = _table_by_column(rel_bias, _bucket_code(np.clip(dist, 0, None), dist >= 0))
    return tiles.reshape(2, n_heads, n_const + 2, tq, tq)


def _const_tile_index(tq):
    exact = N_BUCKETS // 2
    sat = math.ceil(exact * (T5_MAX_DIST / exact) ** ((N_BUCKETS - 1 - exact) / (N_BUCKETS - exact))) + 1
    return -(-(sat + tq - 1) // tq)


def _diff_attn_prompt_call(q, k, v, g, rel_bias, subln, lamc, lam_init, tq):
    b, s, w = q.shape
    hw = 2 * DH_C
    n_heads = w // hw
    n_const = _const_tile_index(tq)
    assert s % (2 * tq) == 0
    tiles = _diff_bias_tiles(rel_bias * LOG2E, tq, n_const, n_heads)
    qspec = pl.BlockSpec((1, 2 * tq, hw), lambda i, h, j, *_: (i, j, h))
    kvspec = pl.BlockSpec((1, s, hw), lambda i, h, j, *_: (i, 0, h), pipeline_mode=pl.Buffered(1))
    in_specs = [qspec, kvspec, kvspec, qspec,
                pl.BlockSpec((2, None, n_const + 2, tq, tq), lambda i, h, j, *_: (0, h, 0, 0, 0),
                             pipeline_mode=pl.Buffered(1)),
                pl.BlockSpec((1, hw), lambda i, h, j, *_: (0, 0)),
                pl.BlockSpec((4, DH_C), lambda i, h, j, *_: (0, 0))]
    return ((b, n_heads, s // (2 * tq)),
            functools.partial(_diff_attn_body, tq=tq, n_const=n_const, lam_init=lam_init),
            (q, k, v, g, tiles, subln.reshape(1, hw), lamc), in_specs, qspec,
            jax.ShapeDtypeStruct((b, s, w), F32), [pltpu.VMEM((s, hw), BF16), pltpu.VMEM((s, 2 * hw), BF16)])


def _hgrn2_gates(q, f_logit, lb, live):
    f = lb + (1.0 - lb) * _sigmoid(f_logit)
    log_f, kk = jnp.log(f), 1.0 - f
    if live is not None:
        log_f, kk = jnp.where(live, log_f, 0.0), jnp.where(live, kk, 0.0)
    return _silu(q) * (DK_D ** -0.5), kk, log_f


def _hgrn2_body(q_ref, f_ref, v_ref, g_ref, lb_ref, gn_ref, s0_ref, y_ref, sl_ref, st_scr, o_scr,
                *, tb, chunk, n_heads):
    j = pl.program_id(1)

    @pl.when(j == 0)
    def _():
        for h in range(n_heads):
            st_scr[h] = s0_ref[0, h].T

    row = lax.broadcasted_iota(jnp.int32, (tb, DK_D), 0)
    pos = row % chunk
    r2 = lax.broadcasted_iota(jnp.int32, (tb, tb), 0)
    c2 = lax.broadcasted_iota(jnp.int32, (tb, tb), 1)
    same_chunk_causal = (r2 // chunk == c2 // chunk) & (c2 <= r2)
    heads = []
    for h in range(n_heads):
        ln = slice(h * DK_D, (h + 1) * DK_D)
        qf, kk, log_f = _hgrn2_gates(q_ref[0, :, ln], f_ref[0, :, ln], lb_ref[:, ln], None)
        v = v_ref[0, :, ln].astype(BF16)
        bc = log_f
        tot = log_f
        s = 1
        while s < chunk:
            bc = jnp.where(pos >= s, bc + pltpu.roll(bc, s, 0), bc)
            tot = tot + jnp.where((pos & s) == 0, pltpu.roll(tot, tb - s, 0), pltpu.roll(tot, s, 0))
            s *= 2
        qd = (qf * jnp.exp(bc)).astype(BF16)
        kd = (kk * jnp.exp(-bc)).astype(BF16)
        kl = (kk * jnp.exp(tot - bc)).astype(BF16)
        att = jnp.where(same_chunk_causal, _dot_nt(qd, kd), 0.0)
        o_scr[:, ln] = _dot(att.astype(BF16), v)
        heads.append((ln, qd, kl, v, jnp.exp(tot), st_scr[h]))

    for n in range(tb // chunk):
        rows = slice(n * chunk, (n + 1) * chunk)
        for h, (ln, qd, kl, v, gdec, st) in enumerate(heads):
            o_scr[rows, ln] += _dot_nt(qd[rows], st.astype(BF16))
            heads[h] = (ln, qd, kl, v, gdec, st * gdec[n * chunk:n * chunk + 1, :] + _dot_tn(v[rows], kl[rows]))

    for h, (ln, _, _, _, _, st) in enumerate(heads):
        st_scr[h] = st
        y_ref[0, :, ln] = _rms(o_scr[:, ln], gn_ref[...]) * _silu(g_ref[0, :, ln])

    @pl.when(j == pl.num_programs(1) - 1)
    def _():
        for h in range(n_heads):
            sl_ref[0, h] = st_scr[h].T


def _hgrn2_prompt(q, f, v, g, lb, gnorm, s0, tb):
    b, s, w = q.shape
    n_heads = w // DK_D
    seq = pl.BlockSpec((1, tb, w), lambda i, j: (i, j, 0))
    st = pl.BlockSpec((1, n_heads, DK_D, DV_D), lambda i, j: (i, 0, 0, 0))
    return pl.pallas_call(
        functools.partial(_hgrn2_body, tb=tb, chunk=CHUNK_D, n_heads=n_heads),
        grid=(b, s // tb),
        in_specs=[seq, seq, seq, seq,
                  pl.BlockSpec((1, w), lambda i, j: (0, 0)),
                  pl.BlockSpec((1, DV_D), lambda i, j: (0, 0)),
                  st],
        out_specs=[seq, st],
        out_shape=[jax.ShapeDtypeStruct((b, s, w), F32),
                   jax.ShapeDtypeStruct((b, n_heads, DK_D, DV_D), F32)],
        scratch_shapes=[pltpu.VMEM((n_heads, DV_D, DK_D), F32), pltpu.VMEM((tb, w), F32)],
        compiler_params=_cparams(("parallel", "arbitrary"), 40),
        name="hgrn2_prompt",
    )(q, f, v, g, lb.reshape(1, w), gnorm.reshape(1, DV_D), s0)


ROWS8 = 8


def _dil_decode_body(q_ref, kn_ref, vn_ref, gb_ref, kt_ref, vt_ref, tab_ref, tabn_ref, y_ref, *, n_heads):
    n_pat = len(DIL_PATTERNS)
    for s_i in range(q_ref.shape[0]):
        for h in range(n_heads):
            q = (q_ref[s_i, h] * (DH_B ** -0.5)).astype(BF16)
            s = _dot(q, kt_ref[s_i, h].astype(BF16))
            sn = _dot_nt(q, kn_ref[s_i, h].astype(BF16))
            parts = [s + tab_ref[h, g * ROWS8:(g + 1) * ROWS8] for g in range(n_pat)]
            parts += [sn + tabn_ref[h, g * ROWS8:(g + 1) * ROWS8, 0:ROWS8] for g in range(n_pat)]
            m = parts[0].max(axis=-1, keepdims=True)
            for p in parts[1:]:
                m = jnp.maximum(m, p.max(axis=-1, keepdims=True))
            ps = [jnp.exp(p - m) for p in parts]
            l = ps[0].sum(axis=-1, keepdims=True)
            for p in ps[1:]:
                l = l + p.sum(axis=-1, keepdims=True)
            pw = (ps[0] + ps[1] + ps[2]).astype(BF16)
            pn = (ps[3] + ps[4] + ps[5]).astype(BF16)
            o = _dot_nt(pw, vt_ref[s_i, h].astype(BF16)) + _dot(pn, vn_ref[s_i, h].astype(BF16))
            y_ref[s_i, h] = (o / l) * _silu(gb_ref[s_i, h])


def _dil_decode_call(q, k, v, gb, win_k, win_v, rel_bias, per_step):
    b, nt, n_heads, dh = q.shape
    l_buf = win_k.shape[1]
    assert nt <= ROWS8 and rel_bias.shape[1] == n_heads
    tt = np.tile(np.arange(ROWS8), len(DIL_PATTERNS))[:, None]
    dil = np.repeat([d for _, d in DIL_PATTERNS], ROWS8)[:, None]
    live = tt < nt

    def code(dd, lo):
        return _bucket_code(np.clip(dd, 0, None), live & (dd % dil == 0) & (dd >= lo * dil) & (dd <= BAND * dil))

    tab = _table_by_column(rel_bias, code(l_buf + tt - np.arange(l_buf)[None, :], 1))
    tn = np.arange(128)[None, :]
    tabn = _table_by_column(rel_bias, np.where(tn < nt, code(tt - tn, 0), -1).astype(np.int32))

    by_head = lambda z: jnp.pad(jnp.transpose(z, (0, 2, 1, 3)), ((0, 0), (0, 0), (0, ROWS8 - nt), (0, 0)))
    pos_minor = lambda z: jnp.transpose(z, (0, 2, 3, 1))
    new = pl.BlockSpec((per_step, n_heads, ROWS8, dh), lambda i: (i, 0, 0, 0))
    win = pl.BlockSpec((per_step, n_heads, dh, l_buf), lambda i: (i, 0, 0, 0))
    const = lambda a: pl.BlockSpec(a.shape, lambda i: (0,) * a.ndim)
    args = (by_head(q), by_head(k), by_head(v), by_head(gb), pos_minor(win_k), pos_minor(win_v), tab, tabn)
    finish = lambda y: jnp.transpose(y[:, :, :nt], (0, 2, 1, 3))
    return (functools.partial(_dil_decode_body, n_heads=n_heads), args,
            [new, new, new, new, win, win, const(tab), const(tabn)], new,
            jax.ShapeDtypeStruct((b, n_heads, ROWS8, dh), F32), finish)


def _fused_body(*refs, first, second, n_in, n_out):
    ins1, ins2 = refs[:n_in[0]], refs[n_in[0]:sum(n_in)]
    outs = refs[sum(n_in):sum(n_in) + sum(n_out)]
    first(*ins1, *outs[:n_out[0]], *refs[sum(n_in) + sum(n_out):])
    second(*ins2, *outs[n_out[0]:])


def _rglru_with_window_decode(rglru, decode):
    grid, r_body, r_args, r_in, r_out, r_shapes, scratch = _rglru_prompt_call(*rglru)
    steps = math.prod(grid)
    b_dec = decode[0].shape[0]
    assert b_dec % steps == 0, "decode sequences must split evenly over the scan's grid steps"
    d_body, d_args, d_in, d_out, d_shape, finish = _dil_decode_call(*decode, b_dec // steps)
    at_step = lambda sp: pl.BlockSpec(sp.block_shape, lambda i, j, f=sp.index_map: f(i * grid[1] + j))
    outs = pl.pallas_call(
        functools.partial(_fused_body, first=r_body, second=d_body, n_in=(len(r_in), len(d_in)),
                          n_out=(len(r_out), 1)),
        grid=grid,
        in_specs=list(r_in) + [at_step(sp) for sp in d_in],
        out_specs=list(r_out) + [at_step(d_out)],
        out_shape=list(r_shapes) + [d_shape],
        scratch_shapes=scratch,
        compiler_params=_cparams(("arbitrary", "arbitrary"), 56),
        name="rglru_window_decode",
    )(*r_args, *d_args)
    return outs[:3], finish(outs[3])


def _diff_decode_body(pt_ref, qm_ref, kn_ref, vn_ref, g_ref, tab_ref, tabn_ref, subln_ref, lamc_ref, *rest,
                      nt, n_heads, n_pages, lam_init):
    del pt_ref
    k_pages, v_pages, y_ref = rest[:n_pages], rest[n_pages:2 * n_pages], rest[2 * n_pages]
    qm = (qm_ref[0] * (DH_C ** -0.5)).astype(BF16)
    rows, hw = qm.shape
    grp = 2 * n_heads
    pw = PAGE * n_heads
    ss = [_dot_nt(qm, k_pages[p][0].astype(BF16)) + tab_ref[:, p * pw:(p + 1) * pw] for p in range(n_pages)]
    sn = _dot_nt(qm, kn_ref[0].astype(BF16)) + tabn_ref[:, 0:kn_ref.shape[1]]
    m = sn.max(axis=-1, keepdims=True)
    for s in ss:
        m = jnp.maximum(m, s.max(axis=-1, keepdims=True))
    ps = [jnp.exp(s - m) for s in ss]
    pn = jnp.exp(sn - m)
    l = pn.sum(axis=-1, keepdims=True)
    for p in ps:
        l = l + p.sum(axis=-1, keepdims=True)
    lam = _lam_from(lamc_ref[...], lam_init)
    row_map = (lax.broadcasted_iota(jnp.int32, (rows, 1), 0) % grp) // n_heads
    coef = jnp.where(row_map == 0, 1.0, -lam) / l
    o = _dot((pn * coef).astype(BF16), vn_ref[0].astype(BF16))
    for p in range(n_pages):
        o = o + _dot((ps[p] * coef).astype(BF16), v_pages[p][0].astype(BF16))
    o = o.reshape(nt, grp, hw)
    oc = o[:, 0:n_heads, :] + o[:, n_heads:grp, :]
    y_ref[0] = _rms(oc, subln_ref[...]) * (1.0 - lam_init) * _silu(g_ref[0])


def _diff_decode_call(q, k, v, g, cache_k, cache_v, page_table, rel_bias, subln, lamc, lam_init):
    b, nt, w = q.shape
    hw = 2 * DH_C
    n_heads = w // hw
    n_phys = cache_k.shape[0]
    n_pages = page_table.shape[1]
    past = n_pages * PAGE
    grp = 2 * n_heads
    rows = nt * grp
    assert rel_bias.shape[1] == grp
    tt = np.arange(rows)[:, None] // grp
    hq = np.arange(rows)[:, None] % n_heads
    pos = lambda n: np.arange(n)[None, :] // n_heads
    same = lambda n: np.arange(n)[None, :] % n_heads == hq
    tab = _table_by_row(rel_bias, _bucket_code(past + tt - pos(past * n_heads), same(past * n_heads)))
    jn = tt - pos(128)
    tabn = _table_by_row(rel_bias, _bucket_code(np.clip(jn, 0, None), same(128) & (jn >= 0) & (pos(128) < nt)))

    map_mask = (np.arange(hw)[None, :] // DH_C == np.arange(2)[:, None]).astype(np.float32)
    qm = q.reshape(b, nt, 1, n_heads, hw) * map_mask[None, None, :, None, :]
    by_head = lambda z: z.reshape(b, nt * n_heads, hw)

    per_b = lambda *shape: pl.BlockSpec((1,) + shape, lambda i, pt: (i,) + (0,) * len(shape))
    const = lambda *shape: pl.BlockSpec(shape, lambda i, pt: (0,) * len(shape))
    page = lambda p: pl.BlockSpec((1, PAGE * n_heads, hw), lambda i, pt, p=p: (pt[i * n_pages + p], 0, 0))
    in_specs = ([per_b(rows, hw), per_b(nt * n_heads, hw), per_b(nt * n_heads, hw), per_b(nt, n_heads, hw),
                 const(rows, past * n_heads), const(rows, 128), const(1, hw), const(4, DH_C)]
                + [page(p) for p in range(n_pages)] * 2)
    rows_of = lambda c: c.reshape(n_phys, PAGE * n_heads, hw)
    args = (qm.reshape(b, rows, hw), by_head(k), by_head(v), g.reshape(b, nt, n_heads, hw),
            tab, tabn, subln.reshape(1, hw), lamc, *([rows_of(cache_k)] * n_pages), *([rows_of(cache_v)] * n_pages))
    return (functools.partial(_diff_decode_body, nt=nt, n_heads=n_heads, n_pages=n_pages, lam_init=lam_init),
            args, in_specs, per_b(nt, n_heads, hw), jax.ShapeDtypeStruct((b, nt, n_heads, hw), F32))


def _diff_attn_both_body(pt_ref, *refs, prompt_body, decode_body, n_prompt_in, n_decode_in):
    n_in = n_prompt_in + n_decode_in
    o_ref, y_ref = refs[n_in], refs[n_in + 1]
    prompt_body(*refs[:n_prompt_in], o_ref, *refs[n_in + 2:])
    decode_body(pt_ref, *refs[n_prompt_in:n_in], y_ref)


def _diff_attn(prompt, decode, page_table, common):
    grid, p_body, p_args, p_specs, p_out, p_shape, scratch = _diff_attn_prompt_call(*prompt, *common)
    d_body, d_args, d_specs, d_out, d_shape = _diff_decode_call(*decode, page_table, *common[:-1])
    assert d_shape.shape[0] == math.prod(grid), "one decode sequence per prompt grid step"

    def at_step(spec):
        seq = lambda i, h, j, pt: (i * grid[1] + h) * grid[2] + j
        return pl.BlockSpec(spec.block_shape, lambda i, h, j, pt, f=spec.index_map: f(seq(i, h, j, pt), pt))

    grid_spec = pltpu.PrefetchScalarGridSpec(
        num_scalar_prefetch=1, grid=grid,
        in_specs=list(p_specs) + [at_step(sp) for sp in d_specs],
        out_specs=[p_out, at_step(d_out)],
        scratch_shapes=scratch)
    return pl.pallas_call(
        functools.partial(_diff_attn_both_body, prompt_body=p_body, decode_body=d_body,
                          n_prompt_in=len(p_specs), n_decode_in=len(d_specs)),
        grid_spec=grid_spec,
        out_shape=[p_shape, d_shape],
        compiler_params=_cparams(("arbitrary", "arbitrary", "arbitrary"), 60),
        name="diff_attn",
    )(page_table.reshape(-1), *p_args, *d_args)


def _hgrn2_decode_body(q_ref, f_ref, ft_ref, v_ref, g_ref, lb_ref, lbc_ref, gn_ref, s0_ref, y_ref, sl_ref,
                       *, nt, n_heads):
    row = lax.broadcasted_iota(jnp.int32, (ROWS8, DK_D), 0)
    r2 = lax.broadcasted_iota(jnp.int32, (ROWS8, ROWS8), 0)
    c2 = lax.broadcasted_iota(jnp.int32, (ROWS8, ROWS8), 1)
    lanes = [slice(h * DK_D, (h + 1) * DK_D) for h in range(n_heads)]
    gated = []
    for ln in lanes:
        qf, kk, log_f = _hgrn2_gates(q_ref[0, :, ln], f_ref[0, :, ln], lb_ref[:, ln], row < nt)
        bc = log_f
        s = 1
        while s < ROWS8:
            bc = jnp.where(row >= s, bc + pltpu.roll(bc, s, 0), bc)
            s *= 2
        tot = bc[ROWS8 - 1:ROWS8, :]
        gated.append(((qf * jnp.exp(bc)).astype(BF16), (kk * jnp.exp(-bc)).astype(BF16),
                      (kk * jnp.exp(tot - bc)).astype(BF16), v_ref[0, :, ln].astype(BF16)))
    states = [s0_ref[0, h] for h in range(n_heads)]
    atts = [jnp.where(c2 <= r2, _dot_nt(qd, kd), 0.0).astype(BF16) for qd, kd, _, _ in gated]
    inter = [_dot(qd, s0.astype(BF16)) for (qd, _, _, _), s0 in zip(gated, states)]
    writes = [_dot_tn(kl, v) for _, _, kl, v in gated]
    outs = [_dot(att, v) + o for att, (_, _, _, v), o in zip(atts, gated, inter)]
    for h, ln in enumerate(lanes):
        lbc = lbc_ref[h]
        fc = lbc + (1.0 - lbc) * _sigmoid(ft_ref[0, h])
        decay = fc[:, 0:1]
        for t in range(1, nt):
            decay = decay * fc[:, t:t + 1]
        sl_ref[0, h] = decay * states[h] + writes[h]
        y_ref[0, :, ln] = _rms(outs[h], gn_ref[...]) * _silu(g_ref[0, :, ln])


def _hgrn2_decode(q, f, v, g, lb, gnorm, s0):
    b, nt, w = q.shape
    n_heads = w // DK_D
    assert nt <= ROWS8
    pad = lambda z: jnp.pad(z, ((0, 0), (0, ROWS8 - nt), (0, 0)))
    cols = jnp.transpose(f.reshape(b, nt, n_heads, DK_D), (0, 2, 3, 1))
    per_b = lambda *shape: pl.BlockSpec((1,) + shape, lambda i: (i,) + (0,) * len(shape))
    const = lambda *shape: pl.BlockSpec(shape, lambda i: (0,) * len(shape))
    y, s_last = pl.pallas_call(
        functools.partial(_hgrn2_decode_body, nt=nt, n_heads=n_heads),
        grid=(b,),
        in_specs=[per_b(ROWS8, w), per_b(ROWS8, w), per_b(n_heads, DK_D, nt), per_b(ROWS8, w), per_b(ROWS8, w),
                  const(1, w), const(n_heads, DK_D, 1), const(1, DV_D), per_b(n_heads, DK_D, DV_D)],
        out_specs=[per_b(ROWS8, w), per_b(n_heads, DK_D, DV_D)],
        out_shape=[jax.ShapeDtypeStruct((b, ROWS8, w), F32),
                   jax.ShapeDtypeStruct((b, n_heads, DK_D, DV_D), F32)],
        compiler_params=_cparams(("parallel",), 32),
        name="hgrn2_decode",
    )(pad(q), pad(f), cols, pad(v), pad(g), lb.reshape(1, w), lb.reshape(n_heads, DK_D, 1), gnorm.reshape(1, DV_D),
      s0)
    return y[:, :nt], s_last


def _layer_ab(xp, xs, group, w_in, gate_params, w_out, rel_bias, state):
    c = w_out.shape[0] // 2
    by_head = lambda z: z.reshape(z.shape[0], z.shape[1], c // DH_B, DH_B)

    def project(x):
        b, t, d = x.shape
        return [z.reshape(b, t, c) for z in _norm_proj(x.reshape(b * t, d), *group, w_in, (c,) * 6, min(512, b * t))]

    def finish(x, ya, yb):
        b, t, d = x.shape
        y = _out_proj(x.reshape(b * t, d), ya.reshape(b * t, c), yb.reshape(b * t, c), w_out, None, min(512, b * t))
        return y.reshape(b, t, d)

    bp, tp, _ = xp.shape
    xa, ga, q, k, v, gb = project(xp)
    xa2, ga2, q2, k2, v2, gb2 = project(xs)
    conv_buf, h0, win_k, win_v = state
    k_rows2, v_rows2 = by_head(k2), by_head(v2)
    (ya, conv_new, h_last), yb2 = _rglru_with_window_decode(
        (xa, ga, jnp.zeros((bp, CONV_W - 1, c), F32), jnp.zeros((bp, c), F32), gate_params, 256),
        (by_head(q2), k_rows2, v_rows2, by_head(gb2), win_k, win_v, rel_bias))
    yb = _dil_prompt(q, k, v, gb, rel_bias)
    ya2, conv_new2, h_last2 = _rglru_decode(xa2, ga2, conv_buf, h0, gate_params)
    keep = min(WIN_MAX, tp)
    return ((finish(xp, ya, yb), conv_new, h_last.reshape(bp, c), by_head(k[:, tp - keep:]), by_head(v[:, tp - keep:])),
            (finish(xs, ya2, yb2), conv_new2, h_last2, k_rows2, v_rows2))


def _layer_cd(xp, xs, group, w_in, lamc, subln, lb, gnorm, w_out, rel_bias, lam_init, final_g, state):
    c = w_out.shape[0] // 2
    heads = c // (2 * DH_C)

    def project(x):
        b, t, d = x.shape
        return [z.reshape(b, t, c) for z in _norm_proj(x.reshape(b * t, d), *group, w_in, (c,) * 8, min(512, b * t))]

    def finish(x, yc, yd, kc, vc, s_last):
        b, t, d = x.shape
        y = _out_proj(x.reshape(b * t, d), yc.reshape(b * t, c), yd.reshape(b * t, c), w_out, final_g,
                      min(512, b * t))
        return (y.reshape(b, t, d), kc.reshape(b, t, heads, 2 * DH_C), vc.reshape(b, t, heads, 2 * DH_C), s_last)

    qc, kc, vc, gc, qd, fd, idd, gd = project(xp)
    qc2, kc2, vc2, gc2, qd2, fd2, idd2, gd2 = project(xs)
    s0, cache_k, cache_v, page_table = state
    yc, yc2 = _diff_attn((qc, kc, vc, gc), (qc2, kc2, vc2, gc2, cache_k, cache_v), page_table,
                         (rel_bias, subln, lamc, lam_init, 256))
    yd, s_last = _hgrn2_prompt(qd, fd, idd, gd, lb, gnorm, jnp.zeros((xp.shape[0], c // DK_D, DK_D, DV_D), F32),
                               tb=256)
    yd2, s_last2 = _hgrn2_decode(qd2, fd2, idd2, gd2, lb, gnorm, s0)
    return finish(xp, yc, yd, kc, vc, s_last), finish(xs, yc2, yd2, kc2, vc2, s_last2)


def kernel(x_prompt, x_sample, state_conv_a, state_h_a, cache_win_k, cache_win_v, cache_k_c, cache_v_c, state_s_d,
           page_table, norm_g, norm_final, rel_bias, w_in_ab, conv_w_a, conv_b_a, w_r_a, b_r_a, w_i_a, b_i_a, lam_a,
           w_out_ab, w_in_cd, lam_c, subln_c, lb_d, gnorm_d, w_out_cd):
    depth = norm_g.shape[0]
    assert depth == 2
    lb_soft = jax.nn.softmax(lb_d.astype(F32), axis=0)
    lb_all = jnp.cumsum(lb_soft, axis=0) - lb_soft[0]

    gate_params = _gate_params(conv_w_a[0], conv_b_a[0], w_r_a[0], b_r_a[0], w_i_a[0], b_i_a[0], lam_a[0])
    (xp, conv_p, h_p, wk_p, wv_p), (xs, conv_s, h_s, wk_s, wv_s) = _layer_ab(
        x_prompt, x_sample, (norm_g[0],), w_in_ab[0], gate_params, w_out_ab[0], rel_bias,
        (state_conv_a[0], state_h_a[0], cache_win_k[0], cache_win_v[0]))
    lam_init = 0.8 - 0.6 * math.exp(-0.3 * 1)
    cd = (w_in_cd[0], lam_c[0], subln_c[0], lb_all[1], gnorm_d[0], w_out_cd[0], rel_bias, lam_init, norm_final)
    (yp, kc_p, vc_p, s_p), (ys, kc_s, vc_s, s_s) = _layer_cd(
        xp, xs, (norm_g[1],), *cd, (state_s_d[0], cache_k_c[0], cache_v_c[0], page_table))
    e = lambda z: z[None]
    return (yp, ys, e(conv_p), e(h_p), e(wk_p), e(wv_p), e(kc_p), e(vc_p), e(s_p),
            e(conv_s), e(h_s), e(wk_s), e(wv_s), e(kc_s), e(vc_s), e(s_s))
```
